```python
import math
import jax, jax.numpy as jnp
from jax import lax
import numpy as np

D_MODEL = 1024
BATCH = 1
SEQ = 16384
DEPTH = 2
DEC_BATCH = 16
DEC_SEQ = 4096
PAST_LEN = 128

GRID_W = 64
N_HEADS = 16
HEAD_DIM = D_MODEL // N_HEADS
WIN_ROWS_MAX = 8
WIN_COLS = 16
SSM_GROUP = 16
N_GROUPS = D_MODEL // SSM_GROUP
STATE = 64
D_FF = -(-8 * D_MODEL // (3 * 256)) * 256
N_MIXERS = 2
N_ATTN = (DEPTH + 1) // 2
N_SSM = DEPTH // 2
EPS = 1e-6
DT_MIN = 1e-3
DT_MAX = 1e-1

kernel_name = 'hybrid_natten_s5_encoder'


def rms_norm(x, g):
    xf = x.astype(jnp.float32)
    y = xf * lax.rsqrt(jnp.mean(xf * xf, axis=-1, keepdims=True) + EPS)
    return (y * g.astype(jnp.float32)).astype(x.dtype)


def neighbourhood_attention(h, w_qkv, w_o, q_gain, k_gain, rpb):
    b, L, _ = h.shape
    rows = L // GRID_W
    kr = min(WIN_ROWS_MAX, rows)
    kc = min(WIN_COLS, GRID_W)
    qkv = h @ w_qkv
    q, k, v = jnp.split(qkv, 3, axis=-1)
    grid = (b, rows, GRID_W, N_HEADS, HEAD_DIM)
    q = rms_norm(q.reshape(grid), q_gain)
    k = rms_norm(k.reshape(grid), k_gain)
    v = v.reshape(grid)
    scale = HEAD_DIM ** -0.5
    col = np.arange(GRID_W)
    cs = np.clip(col - kc // 2, 0, GRID_W - kc)
    col_idx = cs[:, None] + np.arange(kc)
    dc_idx = col_idx - col[:, None] + WIN_COLS - 1
    rpb_c = rpb[:, :, dc_idx]

    def row_block(r):
        rs = jnp.clip(r - kr // 2, 0, rows - kr)
        q_r = lax.dynamic_index_in_dim(q, r, axis=1, keepdims=False)
        k_r = lax.dynamic_slice_in_dim(k, rs, kr, axis=1)[:, :, col_idx]
        v_r = lax.dynamic_slice_in_dim(v, rs, kr, axis=1)[:, :, col_idx]
        dr_idx = rs + jnp.arange(kr) - r + WIN_ROWS_MAX - 1
        bias = jnp.take(rpb_c, dr_idx, axis=1).transpose(2, 0, 1, 3)
        s = jnp.einsum('bqhd,bkqjhd->bqhkj', q_r, k_r,
                       preferred_element_type=jnp.float32) * scale + bias.astype(jnp.float32)[None]
        p = jax.nn.softmax(s.reshape(b, GRID_W, N_HEADS, kr * kc), axis=-1)
        p = p.reshape(s.shape).astype(v.dtype)
        return jnp.einsum('bqhkj,bkqjhd->bqhd', p, v_r)

    o = lax.map(row_block, jnp.arange(rows))
    o = o.transpose(1, 0, 2, 3, 4).reshape(b, L, D_MODEL)
    return o @ w_o


def s5_direction(u, lam_re, lam_im, log_step, b_re, b_im, c_re, c_im):
    dt = jnp.exp(log_step.astype(jnp.float32))[:, None]
    lam_re = lam_re.astype(jnp.float32)
    lam_im = lam_im.astype(jnp.float32)
    mag = jnp.exp(lam_re * dt)
    ang = lam_im * dt
    lb_re = mag * jnp.cos(ang)
    lb_im = mag * jnp.sin(ang)
    den = lam_re * lam_re + lam_im * lam_im
    nr = lb_re - 1.0
    ni = lb_im
    coef_re = (nr * lam_re + ni * lam_im) / den
    coef_im = (ni * lam_re - nr * lam_im) / den
    b_re = b_re.astype(jnp.float32)
    b_im = b_im.astype(jnp.float32)
    bb_re = coef_re[..., None] * b_re - coef_im[..., None] * b_im
    bb_im = coef_re[..., None] * b_im + coef_im[..., None] * b_re
    bu_re = jnp.einsum('lgc,gpc->lgp', u, bb_re)
    bu_im = jnp.einsum('lgc,gpc->lgp', u, bb_im)
    a_re = jnp.broadcast_to(lb_re, bu_re.shape)
    a_im = jnp.broadcast_to(lb_im, bu_im.shape)

    def combine(e1, e2):
        a1r, a1i, b1r, b1i = e1
        a2r, a2i, b2r, b2i = e2
        return (a1r * a2r - a1i * a2i,
                a1r * a2i + a1i * a2r,
                a2r * b1r - a2i * b1i + b2r,
                a2r * b1i + a2i * b1r + b2i)

    _, _, xr, xi = lax.associative_scan(combine, (a_re, a_im, bu_re, bu_im), axis=0)
    return (jnp.einsum('lgp,gcp->lgc', xr, c_re.astype(jnp.float32))
            - jnp.einsum('lgp,gcp->lgc', xi, c_im.astype(jnp.float32)))


def s5_mixer(h, lam_re, lam_im, log_step, b_re, b_im, c_re, c_im, d_skip, w_glu):
    b, L, _ = h.shape
    u = h.astype(jnp.float32).reshape(b, L, N_GROUPS, SSM_GROUP)

    def one_seq(u_s):
        y_f = s5_direction(u_s, lam_re[0], lam_im[0], log_step[0], b_re[0], b_im[0], c_re[0], c_im[0])
        y_b = jnp.flip(s5_direction(jnp.flip(u_s, 0), lam_re[1], lam_im[1], log_step[1],
                                    b_re[1], b_im[1], c_re[1], c_im[1]), 0)
        return y_f + y_b

    y = lax.map(one_seq, u).reshape(b, L, D_MODEL) + d_skip.astype(jnp.float32) * h.astype(jnp.float32)
    z = jax.nn.gelu(y).astype(h.dtype)
    za, zg = jnp.split(z @ w_glu, 2, axis=-1)
    return za * jax.nn.sigmoid(zg)


def swiglu(h, w_gate_up, w_down):
    g, u = jnp.split(h @ w_gate_up, 2, axis=-1)
    return (jax.nn.silu(g) * u) @ w_down


def trunk(x, norm_mix, norm_ffn, w_qkv, w_o, q_gain, k_gain, rpb,
          lam_re, lam_im, log_step, b_re, b_im, c_re, c_im, d_skip, w_glu,
          w_gate_up, w_down):
    for layer in range(DEPTH):
        h = rms_norm(x, norm_mix[layer])
        j = layer // N_MIXERS
        if layer % N_MIXERS == 0:
            x = x + neighbourhood_attention(h, w_qkv[j], w_o[j], q_gain[j], k_gain[j], rpb[j])
        else:
            x = x + s5_mixer(h, lam_re[j], lam_im[j], log_step[j], b_re[j], b_im[j],
                             c_re[j], c_im[j], d_skip[j], w_glu[j])
        h = rms_norm(x, norm_ffn[layer])
        x = x + swiglu(h, w_gate_up[layer], w_down[layer])
    return x


def setup_inputs(seed: int = 0) -> dict:
    key = jax.random.key(seed)
    ks = jax.random.split(key, 20)
    f32 = jnp.float32
    nrm = lambda k, s, sc: jax.random.normal(k, s, f32) * sc
    n_idx = jnp.arange(STATE, dtype=f32)
    lam_re = -0.5 + nrm(ks[9], (N_SSM, 2, N_GROUPS, STATE), 0.01)
    lam_im = math.pi * n_idx + nrm(ks[10], (N_SSM, 2, N_GROUPS, STATE), 0.01)
    log_step = jax.random.uniform(ks[11], (N_SSM, 2, N_GROUPS), f32,
                                  math.log(DT_MIN), math.log(DT_MAX))
    return {
        'x_prompt': nrm(ks[0], (BATCH, SEQ, D_MODEL), 1.0),
        'x_sample': nrm(ks[1], (DEC_BATCH, DEC_SEQ, D_MODEL), 1.0),
        'norm_mix': 1.0 + nrm(ks[2], (DEPTH, D_MODEL), 0.02),
        'norm_ffn': 1.0 + nrm(ks[3], (DEPTH, D_MODEL), 0.02),
        'w_qkv': nrm(ks[4], (N_ATTN, D_MODEL, 3 * D_MODEL), D_MODEL ** -0.5),
        'w_o': nrm(ks[5], (N_ATTN, D_MODEL, D_MODEL), D_MODEL ** -0.5),
        'q_gain': 1.0 + nrm(ks[6], (N_ATTN, HEAD_DIM), 0.02),
        'k_gain': 1.0 + nrm(ks[7], (N_ATTN, HEAD_DIM), 0.02),
        'rpb': nrm(ks[8], (N_ATTN, N_HEADS, 2 * WIN_ROWS_MAX - 1, 2 * WIN_COLS - 1), 0.1),
        'lam_re': lam_re,
        'lam_im': lam_im,
        'log_step': log_step,
        'b_re': nrm(ks[12], (N_SSM, 2, N_GROUPS, STATE, SSM_GROUP), (2 * SSM_GROUP) ** -0.5),
        'b_im': nrm(ks[13], (N_SSM, 2, N_GROUPS, STATE, SSM_GROUP), (2 * SSM_GROUP) ** -0.5),
        'c_re': nrm(ks[14], (N_SSM, 2, N_GROUPS, SSM_GROUP, STATE), (2 * STATE) ** -0.5),
        'c_im': nrm(ks[15], (N_SSM, 2, N_GROUPS, SSM_GROUP, STATE), (2 * STATE) ** -0.5),
        'd_skip': nrm(ks[16], (N_SSM, D_MODEL), 0.5),
        'w_glu': nrm(ks[17], (N_SSM, D_MODEL, 2 * D_MODEL), D_MODEL ** -0.5),
        'w_gate_up': nrm(ks[18], (DEPTH, D_MODEL, 2 * D_FF), D_MODEL ** -0.5),
        'w_down': nrm(ks[19], (DEPTH, D_FF, D_MODEL), D_FF ** -0.5),
    }


def reference(x_prompt, x_sample, norm_mix, norm_ffn, w_qkv, w_o, q_gain, k_gain, rpb,
              lam_re, lam_im, log_step, b_re, b_im, c_re, c_im, d_skip, w_glu,
              w_gate_up, w_down):
    y_prompt = trunk(x_prompt, norm_mix, norm_ffn, w_qkv, w_o, q_gain, k_gain, rpb,
                     lam_re, lam_im, log_step, b_re, b_im, c_re, c_im, d_skip, w_glu,
                     w_gate_up, w_down)
    y_sample = trunk(x_sample, norm_mix, norm_ffn, w_qkv, w_o, q_gain, k_gain, rpb,
                     lam_re, lam_im, log_step, b_re, b_im, c_re, c_im, d_skip, w_glu,
                     w_gate_up, w_down)
    return (y_prompt, y_sample)
```

```python
import functools

import numpy as np
import jax
import jax.numpy as jnp
from jax import lax
from jax.experimental import pallas as pl
from jax.experimental.pallas import tpu as pltpu

D_MODEL = 1024
GRID_W = 64
N_HEADS = 16
HEAD_DIM = D_MODEL // N_HEADS
WIN_ROWS = 8
WIN_COLS = 16
SSM_GROUP = 16
N_GROUPS = D_MODEL // SSM_GROUP
STATE = 64
D_FF = 2816
EPS = 1e-6

F32 = jnp.float32
BF16 = jnp.bfloat16

V7X_LANES = 128
V7X_VMEM_BYTES = 64 * 1024 * 1024
VMEM_LIMIT = (V7X_VMEM_BYTES * 3) // 4

HEAD_PAIRS = N_HEADS // 2
PAIR_W = 2 * HEAD_DIM
Q_ROWS = 4
QB = Q_ROWS * GRID_W
KB = 3 * QB
NEG = -1e30

CHUNK = 16
CW = CHUNK * SSM_GROUP
SW = 2 * STATE
GROUP_BLOCK = 16
GB_W = GROUP_BLOCK * SSM_GROUP
N_GB = N_GROUPS // GROUP_BLOCK
FF_CHUNK = D_FF // 2


def _cparams(sem):
    return pltpu.CompilerParams(dimension_semantics=sem, vmem_limit_bytes=VMEM_LIMIT)


def _rms(x, g):
    return x * lax.rsqrt(jnp.mean(x * x, axis=-1, keepdims=True) + EPS) * g


def _dot(a, b):
    return jnp.dot(a, b, preferred_element_type=F32)


def _dot_nt(a, b):
    return lax.dot_general(a, b, (((1,), (1,)), ((), ())), preferred_element_type=F32)


def _dot_tn(a, b):
    return lax.dot_general(a, b, (((0,), (0,)), ((), ())), preferred_element_type=F32)


def _qkv_kernel(x_ref, g_ref, w_ref, gq_ref, gk_ref, q_ref, k_ref, v_ref):
    x = x_ref[0]
    tm = x.shape[0]
    h = _rms(x, g_ref[...]).astype(BF16)
    qkv = _dot_nt(w_ref[...], h)

    def head_norm(t, gain):
        t3 = t.reshape(N_HEADS, HEAD_DIM, tm)
        y = t3 * lax.rsqrt(jnp.mean(t3 * t3, axis=1, keepdims=True) + EPS)
        return y * gain[None]

    q = head_norm(qkv[0:D_MODEL], gq_ref[...])
    k = head_norm(qkv[D_MODEL:2 * D_MODEL], gk_ref[...])
    v = qkv[2 * D_MODEL:3 * D_MODEL]
    q_ref[0] = q.reshape(HEAD_PAIRS, PAIR_W, tm).astype(BF16)
    v_ref[0] = v.reshape(HEAD_PAIRS, PAIR_W, tm).astype(BF16)
    kt = k.reshape(D_MODEL, tm).T
    for j in range(HEAD_PAIRS):
        k_ref[0, j] = kt[:, PAIR_W * j:PAIR_W * (j + 1)].astype(BF16)


def _qkv_call(x, g, w_t, gq, gk, tm):
    b, l, _ = x.shape
    return pl.pallas_call(
        _qkv_kernel,
        grid=(b, l // tm),
        in_specs=[
            pl.BlockSpec((1, tm, D_MODEL), lambda i, j: (i, j, 0)),
            pl.BlockSpec((1, D_MODEL), lambda i, j: (0, 0)),
            pl.BlockSpec((3 * D_MODEL, D_MODEL), lambda i, j: (0, 0)),
            pl.BlockSpec((HEAD_DIM, tm), lambda i, j: (0, 0)),
            pl.BlockSpec((HEAD_DIM, tm), lambda i, j: (0, 0)),
        ],
        out_specs=[
            pl.BlockSpec((1, HEAD_PAIRS, PAIR_W, tm), lambda i, j: (i, 0, 0, j)),
            pl.BlockSpec((1, HEAD_PAIRS, tm, PAIR_W), lambda i, j: (i, 0, j, 0)),
            pl.BlockSpec((1, HEAD_PAIRS, PAIR_W, tm), lambda i, j: (i, 0, 0, j)),
        ],
        out_shape=[
            jax.ShapeDtypeStruct((b, HEAD_PAIRS, PAIR_W, l), BF16),
            jax.ShapeDtypeStruct((b, HEAD_PAIRS, l, PAIR_W), BF16),
            jax.ShapeDtypeStruct((b, HEAD_PAIRS, PAIR_W, l), BF16),
        ],
        compiler_params=_cparams(("parallel", "parallel")),
        name="qkv_proj",
    )(x, g, w_t, gq, gk)


def _attn_kernel(q_ref, kp_ref, kc_ref, kn_ref, vp_ref, vc_ref, vn_ref, hm_ref, bias_ref, rm_ref, o_ref):
    rm = rm_ref[0]

    def pair_body(j, carry):
        q2 = q_ref[0, j]
        kp, kc, kn = kp_ref[0, j], kc_ref[0, j], kn_ref[0, j]
        vp, vc, vn = vp_ref[0, j], vc_ref[0, j], vn_ref[0, j]
        for e in range(2):
            qm = q2 * hm_ref[e]
            s = jnp.concatenate([_dot(kp, qm), _dot(kc, qm), _dot(kn, qm)], axis=0)
            s = s + bias_ref[j, e] + rm
            m = jnp.max(s, axis=0, keepdims=True)
            p = jnp.exp(s - m)
            l = jnp.sum(p, axis=0, keepdims=True)
            pb = p.astype(BF16)
            lo, hi = HEAD_DIM * e, HEAD_DIM * (e + 1)
            o = (_dot(vp[lo:hi], pb[0:QB]) + _dot(vc[lo:hi], pb[QB:2 * QB])
                 + _dot(vn[lo:hi], pb[2 * QB:3 * QB]))
            o_ref[0, j, lo:hi, :] = (o / l).astype(BF16)
        return carry

    lax.fori_loop(0, HEAD_PAIRS, pair_body, 0)


def _attn_call(q_t, k, v_t, hm, bias, rmask):
    b, _, _, l = q_t.shape
    nblk = l // QB
    prev = lambda j: jnp.maximum(j - 1, 0)
    nxt = lambda j: jnp.minimum(j + 1, nblk - 1)
    variant = lambda j: (j > 0).astype(jnp.int32) + (j == nblk - 1).astype(jnp.int32)
    kspec = lambda f: pl.BlockSpec((1, HEAD_PAIRS, QB, PAIR_W), lambda i, j: (i, 0, f(j), 0))
    vspec = lambda f: pl.BlockSpec((1, HEAD_PAIRS, PAIR_W, QB), lambda i, j: (i, 0, 0, f(j)))
    same = lambda j: j
    return pl.pallas_call(
        _attn_kernel,
        grid=(b, nblk),
        in_specs=[
            vspec(same),
            kspec(prev), kspec(same), kspec(nxt),
            vspec(prev), vspec(same), vspec(nxt),
            pl.BlockSpec((2, PAIR_W, QB), lambda i, j: (0, 0, 0)),
            pl.BlockSpec((HEAD_PAIRS, 2, KB, QB), lambda i, j: (0, 0, 0, 0), pipeline_mode=pl.Buffered(1)),
            pl.BlockSpec((1, KB, QB), lambda i, j: (variant(j), 0, 0)),
        ],
        out_specs=vspec(same),
        out_shape=jax.ShapeDtypeStruct((b, HEAD_PAIRS, PAIR_W, l), BF16),
        compiler_params=_cparams(("parallel", "parallel")),
        name="nbr_attention",
    )(q_t, k, k, k, v_t, v_t, v_t, hm, bias, rmask)


def _attn_tables(rpb):
    kr = np.arange(3 * Q_ROWS)[:, None, None, None]
    kc = np.arange(GRID_W)[None, :, None, None]
    r = np.arange(Q_ROWS)[None, None, :, None]
    c = np.arange(GRID_W)[None, None, None, :]
    shape = (3 * Q_ROWS, GRID_W, Q_ROWS, GRID_W)
    dr_idx = np.broadcast_to(kr - Q_ROWS - r + WIN_ROWS - 1, shape)
    cs = np.clip(c - WIN_COLS // 2, 0, GRID_W - WIN_COLS)
    col_ok = np.broadcast_to((kc >= cs) & (kc < cs + WIN_COLS), shape)
    dc_idx = np.broadcast_to(np.clip(kc - c + WIN_COLS - 1, 0, 2 * WIN_COLS - 2), shape)
    bias = rpb.astype(F32)[:, dr_idx, dc_idx]
    bias = jnp.where(col_ok[None], bias, NEG).reshape(HEAD_PAIRS, 2, KB, QB)
    first = np.broadcast_to(kr >= Q_ROWS, shape)
    inner = np.broadcast_to((kr >= r) & (kr < r + WIN_ROWS), shape)
    last = np.broadcast_to(kr < 2 * Q_ROWS, shape)
    rmask = np.where(np.stack([first, inner, last]), 0.0, NEG).astype(np.float32).reshape(3, KB, QB)
    return bias, jnp.asarray(rmask)


def _ffn_steps(c, x0_sc, hb_sc, acc_sc, wg_ref, wu_ref, wd_ref):
    hb = hb_sc[...]
    g = _dot(hb, wg_ref[...])
    u = _dot(hb, wu_ref[...])
    act = (g * (1.0 / (1.0 + jnp.exp(-g))) * u).astype(BF16)
    y = _dot(act, wd_ref[...])

    @pl.when(c == 0)
    def _():
        acc_sc[...] = y

    @pl.when(c != 0)
    def _():
        acc_sc[...] += y


def _attn_ffn_kernel(x_ref, o_ref, wo_ref, gf_ref, wg_ref, wu_ref, wd_ref, gn_ref,
                     x1_ref, h1_ref, x0_sc, hb_sc, acc_sc):
    c = pl.program_id(2)

    @pl.when(c == 0)
    def _():
        o = o_ref[0].reshape(D_MODEL, -1)
        x0 = x_ref[0] + _dot_tn(o, wo_ref[...])
        x0_sc[...] = x0
        hb_sc[...] = _rms(x0, gf_ref[...]).astype(BF16)

    _ffn_steps(c, x0_sc, hb_sc, acc_sc, wg_ref, wu_ref, wd_ref)

    @pl.when(c == pl.num_programs(2) - 1)
    def _():
        x1 = x0_sc[...] + acc_sc[...]
        x1_ref[0] = x1
        h1_ref[0] = _rms(x1, gn_ref[...])


def _ffn_weight_specs(nlead):
    nff = D_FF // FF_CHUNK
    pick = lambda *idx: idx[nlead]
    return [
        pl.BlockSpec((D_MODEL, FF_CHUNK), lambda *idx: (0, pick(*idx))),
        pl.BlockSpec((D_MODEL, FF_CHUNK), lambda *idx: (0, pick(*idx) + nff)),
        pl.BlockSpec((FF_CHUNK, D_MODEL), lambda *idx: (pick(*idx), 0)),
    ]


def _attn_ffn_call(x, o_t, wo, gf, wgu, wd, gn, tm):
    b, l, _ = x.shape
    nff = D_FF // FF_CHUNK
    row = pl.BlockSpec((1, tm, D_MODEL), lambda i, j, c: (i, j, 0))
    vec = pl.BlockSpec((1, D_MODEL), lambda i, j, c: (0, 0))
    return pl.pallas_call(
        _attn_ffn_kernel,
        grid=(b, l // tm, nff),
        in_specs=[
            row,
            pl.BlockSpec((1, HEAD_PAIRS, PAIR_W, tm), lambda i, j, c: (i, 0, 0, j)),
            pl.BlockSpec((D_MODEL, D_MODEL), lambda i, j, c: (0, 0)),
            vec,
            *_ffn_weight_specs(2),
            vec,
        ],
        out_specs=[row, row],
        out_shape=[jax.ShapeDtypeStruct((b, l, D_MODEL), F32)] * 2,
        scratch_shapes=[
            pltpu.VMEM((tm, D_MODEL), F32),
            pltpu.VMEM((tm, D_MODEL), BF16),
            pltpu.VMEM((tm, D_MODEL), F32),
        ],
        compiler_params=_cparams(("parallel", "parallel", "arbitrary")),
        name="attn_out_ffn",
    )(x, o_t, wo, gf, wgu, wgu, wd, gn)


def _glu_ffn_kernel(x_ref, z_ref, wglu_ref, gf_ref, wg_ref, wu_ref, wd_ref,
                    y_ref, x0_sc, hb_sc, acc_sc):
    c = pl.program_id(2)

    @pl.when(c == 0)
    def _():
        zz = _dot(z_ref[0], wglu_ref[...])
        za, zg = zz[:, :D_MODEL], zz[:, D_MODEL:]
        x0 = x_ref[...] + za * (1.0 / (1.0 + jnp.exp(-zg)))
        x0_sc[...] = x0
        hb_sc[...] = _rms(x0, gf_ref[...]).astype(BF16)

    _ffn_steps(c, x0_sc, hb_sc, acc_sc, wg_ref, wu_ref, wd_ref)

    @pl.when(c == pl.num_programs(2) - 1)
    def _():
        y_ref[...] = x0_sc[...] + acc_sc[...]


def _glu_ffn_call(x_ph, z_pm, wglu, gf, wgu, wd, tm):
    nc = x_ph.shape[0]
    nff = D_FF // FF_CHUNK
    row = pl.BlockSpec((tm, D_MODEL), lambda t, i, c: (i, t))
    vec = pl.BlockSpec((1, D_MODEL), lambda t, i, c: (0, 0))
    return pl.pallas_call(
        _glu_ffn_kernel,
        grid=(CHUNK, nc // tm, nff),
        in_specs=[
            row,
            pl.BlockSpec((1, tm, D_MODEL), lambda t, i, c: (t, i, 0)),
            pl.BlockSpec((D_MODEL, 2 * D_MODEL), lambda t, i, c: (0, 0)),
            vec,
            *_ffn_weight_specs(2),
        ],
        out_specs=row,
        out_shape=jax.ShapeDtypeStruct(x_ph.shape, F32),
        scratch_shapes=[
            pltpu.VMEM((tm, D_MODEL), F32),
            pltpu.VMEM((tm, D_MODEL), BF16),
            pltpu.VMEM((tm, D_MODEL), F32),
        ],
        compiler_params=_cparams(("parallel", "parallel", "arbitrary")),
        name="glu_out_ffn",
    )(x_ph, z_pm, wglu, gf, wgu, wgu, wd)


def _chunk_operands(h_refs, ut_sc, ht_sc):
    for s, h_ref in enumerate(h_refs):
        ht = h_ref[...].T
        if ht_sc is not None:
            ht_sc[s] = ht
        ut_sc[:, SSM_GROUP * s:SSM_GROUP * (s + 1), :] = (
            ht.reshape(GROUP_BLOCK, SSM_GROUP, -1).astype(BF16))


def _s5_state_kernel(*refs):
    h_refs = refs[:CHUNK]
    pt_ref, sf_ref, sb_ref, ut_sc = refs[CHUNK:]
    _chunk_operands(h_refs, ut_sc, None)
    for g in range(GROUP_BLOCK):
        st = _dot(pt_ref[g], ut_sc[g]).T
        sf_ref[:, SW * g:SW * (g + 1)] = st[:, :SW]
        sb_ref[:, SW * g:SW * (g + 1)] = st[:, SW:]


def _h_specs(nk):
    return [pl.BlockSpec((nk, GB_W), lambda gb, i, s=s: (i, s * N_GB + gb)) for s in range(CHUNK)]


def _s5_state_call(h_ph, pt, nk):
    nc = h_ph.shape[0]
    sspec = pl.BlockSpec((nk, GROUP_BLOCK * SW), lambda gb, i: (i, gb))
    return pl.pallas_call(
        _s5_state_kernel,
        grid=(N_GB, nc // nk),
        in_specs=[*_h_specs(nk),
                  pl.BlockSpec((GROUP_BLOCK, 2 * SW, CW), lambda gb, i: (gb, 0, 0))],
        out_specs=[sspec, sspec],
        out_shape=[jax.ShapeDtypeStruct((nc, N_GROUPS * SW), F32)] * 2,
        scratch_shapes=[pltpu.VMEM((GROUP_BLOCK, CW, nk), BF16)],
        compiler_params=_cparams(("parallel", "parallel")),
        name="s5_chunk_states",
    )(*([h_ph] * CHUNK), pt)


def _scan_kernel(s_ref, ar_ref, ai_ref, xin_ref, x_sc, xs_sc, *, reverse):
    kc = s_ref.shape[1]

    @pl.when(pl.program_id(1) == 0)
    def _():
        x_sc[...] = jnp.zeros_like(x_sc)
        xs_sc[...] = jnp.zeros_like(xs_sc)

    ar = ar_ref[...]
    ai = ai_ref[...]

    def body(i, carry):
        x, xs = carry
        k = kc - 1 - i if reverse else i
        s = s_ref[0, k]
        xin_ref[0, k] = x.astype(BF16)
        ss = pltpu.roll(s, STATE, 1)
        return ar * x + ai * xs + s, ar * xs - ai * x + ss

    x, xs = lax.fori_loop(0, kc, body, (x_sc[...], xs_sc[...]))
    x_sc[...] = x
    xs_sc[...] = xs


def _scan_call(s, ar, ai, kc, reverse):
    b, ncs = s.shape[:2]
    nkb = ncs // kc
    blk = (lambda i, j: (i, nkb - 1 - j, 0, 0)) if reverse else (lambda i, j: (i, j, 0, 0))
    tab = pl.BlockSpec((N_GROUPS, SW), lambda i, j: (0, 0))
    return pl.pallas_call(
        functools.partial(_scan_kernel, reverse=reverse),
        grid=(b, nkb),
        in_specs=[pl.BlockSpec((1, kc, N_GROUPS, SW), blk), tab, tab],
        out_specs=pl.BlockSpec((1, kc, N_GROUPS, SW), blk),
        out_shape=jax.ShapeDtypeStruct(s.shape, BF16),
        scratch_shapes=[pltpu.VMEM((N_GROUPS, SW), F32)] * 2,
        compiler_params=_cparams(("parallel", "arbitrary")),
        name="s5_chunk_scan_bwd" if reverse else "s5_chunk_scan_fwd",
    )(s, ar, ai)


def _s5_out_kernel(*refs):
    h_refs = refs[:CHUNK]
    xf_ref, xb_ref, mt_ref, qt_ref, d_ref, z_ref, ut_sc, ht_sc, yt_sc = refs[CHUNK:]
    _chunk_operands(h_refs, ut_sc, ht_sc)
    for g in range(GROUP_BLOCK):
        xin = jnp.concatenate([xf_ref[:, SW * g:SW * (g + 1)], xb_ref[:, SW * g:SW * (g + 1)]], axis=1)
        yt_sc[g] = _dot(mt_ref[g], ut_sc[g]) + _dot_nt(qt_ref[g], xin)
    d = d_ref[...]
    for t in range(CHUNK):
        y = yt_sc[:, SSM_GROUP * t:SSM_GROUP * (t + 1), :].reshape(GB_W, -1)
        v = y + d * ht_sc[t]
        z_ref[t] = jax.nn.gelu(v, approximate=True).T.astype(BF16)


def _s5_out_call(h_ph, xin_f, xin_b, mt, qt, d_col, nk):
    nc = h_ph.shape[0]
    xspec = pl.BlockSpec((nk, GROUP_BLOCK * SW), lambda gb, i: (i, gb))
    wspec = pl.BlockSpec((GROUP_BLOCK, CW, CW), lambda gb, i: (gb, 0, 0))
    return pl.pallas_call(
        _s5_out_kernel,
        grid=(N_GB, nc // nk),
        in_specs=[*_h_specs(nk), xspec, xspec, wspec, wspec,
                  pl.BlockSpec((GB_W, nk), lambda gb, i: (gb, 0))],
        out_specs=pl.BlockSpec((CHUNK, nk, GB_W), lambda gb, i: (0, i, gb)),
        out_shape=jax.ShapeDtypeStruct((CHUNK, nc, D_MODEL), BF16),
        scratch_shapes=[
            pltpu.VMEM((GROUP_BLOCK, CW, nk), BF16),
            pltpu.VMEM((CHUNK, GB_W, nk), F32),
            pltpu.VMEM((GROUP_BLOCK, CW, nk), F32),
        ],
        compiler_params=_cparams(("parallel", "parallel")),
        name="s5_outputs",
    )(*([h_ph] * CHUNK), xin_f, xin_b, mt, qt, d_col)


def _s5_matrices(lam_re, lam_im, log_step, b_re, b_im, c_re, c_im):
    hp = lax.Precision.HIGHEST
    dt = jnp.exp(log_step.astype(F32))[..., None]
    lam_re = lam_re.astype(F32)
    lam_im = lam_im.astype(F32)
    mag = jnp.exp(lam_re * dt)
    ang = lam_im * dt
    lb_re = mag * jnp.cos(ang)
    lb_im = mag * jnp.sin(ang)
    den = lam_re * lam_re + lam_im * lam_im
    nr = lb_re - 1.0
    ni = lb_im
    coef_re = (nr * lam_re + ni * lam_im) / den
    coef_im = (ni * lam_re - nr * lam_im) / den
    b_re = b_re.astype(F32)
    b_im = b_im.astype(F32)
    bb_re = coef_re[..., None] * b_re - coef_im[..., None] * b_im
    bb_im = coef_re[..., None] * b_im + coef_im[..., None] * b_re
    c_re = c_re.astype(F32)
    c_im = c_im.astype(F32)

    pw_re = [jnp.ones_like(lb_re)]
    pw_im = [jnp.zeros_like(lb_im)]
    for _ in range(CHUNK):
        pr, pi = pw_re[-1], pw_im[-1]
        pw_re.append(pr * lb_re - pi * lb_im)
        pw_im.append(pr * lb_im + pi * lb_re)
    pw_re = jnp.stack(pw_re)
    pw_im = jnp.stack(pw_im)

    w_re = pw_re[:CHUNK, ..., None] * bb_re - pw_im[:CHUNK, ..., None] * bb_im
    w_im = pw_re[:CHUNK, ..., None] * bb_im + pw_im[:CHUNK, ..., None] * bb_re
    kern = (jnp.einsum('dgcn,tdgnk->tdgck', c_re, w_re, precision=hp)
            - jnp.einsum('dgcn,tdgnk->tdgck', c_im, w_im, precision=hp))

    s_i = np.arange(CHUNK)[:, None]
    t_i = np.arange(CHUNK)[None, :]
    kf = jnp.where((t_i >= s_i)[..., None, None, None], kern[np.clip(t_i - s_i, 0, CHUNK - 1), 0], 0.0)
    kb = jnp.where((s_i >= t_i)[..., None, None, None], kern[np.clip(s_i - t_i, 0, CHUNK - 1), 1], 0.0)
    m = kf + kb
    mt = m.transpose(2, 1, 3, 0, 4).reshape(N_GROUPS, CW, CW)

    rev = np.arange(CHUNK)[::-1]
    pf = jnp.stack([w_re[rev, 0], w_im[rev, 0]])
    pb = jnp.stack([w_re[:, 1], w_im[:, 1]])
    p_all = jnp.stack([pf, pb])
    pt = p_all.transpose(3, 0, 1, 4, 2, 5).reshape(N_GROUPS, 2 * SW, CW)

    def carry(d, taus):
        cr = c_re[d][None]
        ci = c_im[d][None]
        pr = pw_re[taus, d][:, :, None, :]
        pi = pw_im[taus, d][:, :, None, :]
        return jnp.stack([cr * pr - ci * pi, -(cr * pi + ci * pr)])

    q_all = jnp.stack([carry(0, np.arange(1, CHUNK + 1)), carry(1, CHUNK - np.arange(CHUNK))])
    qt = q_all.transpose(3, 2, 4, 0, 1, 5).reshape(N_GROUPS, CW, 2 * SW)

    a_re = pw_re[CHUNK]
    a_im = pw_im[CHUNK]
    ar = jnp.concatenate([a_re, a_re], axis=-1)
    ai = jnp.concatenate([-a_im, a_im], axis=-1)
    return mt.astype(BF16), pt.astype(BF16), qt.astype(BF16), ar, ai


def _trunk(x, p, tm, nk):
    b, l, _ = x.shape
    nc = b * l // CHUNK
    kc = min(256, l // CHUNK)
    q_t, k, v_t = _qkv_call(x, p['g_mix0'], p['wqkv_t'], p['gq'], p['gk'], tm)
    o_t = _attn_call(q_t, k, v_t, p['hm'], p['bias'], p['rmask'])
    x1, h1 = _attn_ffn_call(x, o_t, p['wo'], p['g_ffn0'], p['wgu0'], p['wd0'], p['g_mix1'], tm)

    h_ph = h1.reshape(nc, CHUNK * D_MODEL)
    sf, sb = _s5_state_call(h_ph, p['pt'], nk)
    sshape = (b, l // CHUNK, N_GROUPS, SW)
    xin_f = _scan_call(sf.reshape(sshape), p['ar'][0], p['ai'][0], kc, False)
    xin_b = _scan_call(sb.reshape(sshape), p['ar'][1], p['ai'][1], kc, True)
    z_pm = _s5_out_call(h_ph, xin_f.reshape(nc, N_GROUPS * SW), xin_b.reshape(nc, N_GROUPS * SW),
                        p['mt'], p['qt'], p['d_col'], nk)
    y = _glu_ffn_call(x1.reshape(nc, CHUNK * D_MODEL), z_pm, p['wglu'], p['g_ffn1'], p['wgu1'], p['wd1'], min(tm, nc))
    return y.reshape(b, l, D_MODEL)


def kernel(x_prompt, x_sample, norm_mix, norm_ffn, w_qkv, w_o, q_gain, k_gain, rpb, lam_re, lam_im, log_step,
           b_re, b_im, c_re, c_im, d_skip, w_glu, w_gate_up, w_down):
    tm, nk = 512, 128
    scale = HEAD_DIM ** -0.5
    mt, pt, qt, ar, ai = _s5_matrices(lam_re[0], lam_im[0], log_step[0], b_re[0], b_im[0], c_re[0], c_im[0])
    bias, rmask = _attn_tables(rpb[0])
    rows = np.arange(PAIR_W)[None, :, None]
    hm = np.broadcast_to((rows // HEAD_DIM) == np.arange(2)[:, None, None], (2, PAIR_W, QB))
    p = dict(
        g_mix0=norm_mix[0].astype(F32)[None], g_mix1=norm_mix[1].astype(F32)[None],
        g_ffn0=norm_ffn[0].astype(F32)[None], g_ffn1=norm_ffn[1].astype(F32)[None],
        wqkv_t=w_qkv[0].T.astype(BF16), wo=w_o[0].astype(BF16),
        gq=jnp.broadcast_to((q_gain[0].astype(F32) * scale)[:, None], (HEAD_DIM, tm)),
        gk=jnp.broadcast_to(k_gain[0].astype(F32)[:, None], (HEAD_DIM, tm)),
        hm=jnp.asarray(hm, BF16), bias=bias, rmask=rmask,
        wgu0=w_gate_up[0].astype(BF16), wd0=w_down[0].astype(BF16),
        wgu1=w_gate_up[1].astype(BF16), wd1=w_down[1].astype(BF16),
        wglu=w_glu[0].astype(BF16),
        mt=mt, pt=pt, qt=qt, ar=ar, ai=ai,
        d_col=jnp.broadcast_to(d_skip[0].astype(F32)[:, None], (D_MODEL, nk)),
    )
    return _trunk(x_prompt, p, tm, nk), _trunk(x_sample, p, tm, nk)
```

```python
import functools

import numpy as np
import jax
import jax.numpy as jnp
from jax import lax
from jax.experimental import pallas as pl
from jax.experimental.pallas import tpu as pltpu

D_MODEL = 1024
GRID_W = 64
N_HEADS = 16
HEAD_DIM = D_MODEL // N_HEADS
WIN_ROWS = 8
WIN_COLS = 16
SSM_GROUP = 16
N_GROUPS = D_MODEL // SSM_GROUP
STATE = 64
D_FF = 2816
EPS = 1e-6

F32 = jnp.float32
BF16 = jnp.bfloat16

V7X_LANES = 128
V7X_VMEM_BYTES = 64 * 1024 * 1024
VMEM_LIMIT = (V7X_VMEM_BYTES * 3) // 4

HEAD_PAIRS = N_HEADS // 2
PAIR_W = 2 * HEAD_DIM
Q_ROWS = 4
QB = Q_ROWS * GRID_W
KB = 3 * QB
NEG = -1e30
ONES_ROWS = 16
LOG2E = 1.4426950408889634

CHUNK = 16
CW = CHUNK * SSM_GROUP
SW = 2 * STATE
GROUP_BLOCK = 16
GB_W = GROUP_BLOCK * SSM_GROUP
N_GB = N_GROUPS // GROUP_BLOCK
FF_CHUNK = D_FF // 2


def _cparams(sem):
    return pltpu.CompilerParams(dimension_semantics=sem, vmem_limit_bytes=VMEM_LIMIT)


def _rms(x, g):
    return x * lax.rsqrt(jnp.mean(x * x, axis=-1, keepdims=True) + EPS) * g


def _dot(a, b):
    return jnp.dot(a, b, preferred_element_type=F32)


def _dot_nt(a, b):
    return lax.dot_general(a, b, (((1,), (1,)), ((), ())), preferred_element_type=F32)


def _dot_tn(a, b):
    return lax.dot_general(a, b, (((0,), (0,)), ((), ())), preferred_element_type=F32)


def _qkv_kernel(x_ref, g_ref, w_ref, gq_ref, gk_ref, q_ref, k_ref, v_ref):
    x = x_ref[0]
    tm = x.shape[0]
    h = _rms(x, g_ref[...]).astype(BF16)
    qkv = _dot_nt(w_ref[...], h)

    def head_norm(t, gain):
        t3 = t.reshape(N_HEADS, HEAD_DIM, tm)
        y = t3 * lax.rsqrt(jnp.mean(t3 * t3, axis=1, keepdims=True) + EPS)
        return y * gain[None]

    q = head_norm(qkv[0:D_MODEL], gq_ref[...])
    k = head_norm(qkv[D_MODEL:2 * D_MODEL], gk_ref[...])
    v = qkv[2 * D_MODEL:3 * D_MODEL]
    q_ref[0] = q.reshape(HEAD_PAIRS, PAIR_W, tm).astype(BF16)
    v_ref[0] = v.reshape(HEAD_PAIRS, PAIR_W, tm).astype(BF16)
    kt = k.reshape(D_MODEL, tm).T
    for j in range(HEAD_PAIRS):
        k_ref[0, j] = kt[:, PAIR_W * j:PAIR_W * (j + 1)].astype(BF16)


def _qkv_call(x, g, w_t, gq, gk, tm):
    b, l, _ = x.shape
    return pl.pallas_call(
        _qkv_kernel,
        grid=(b, l // tm),
        in_specs=[
            pl.BlockSpec((1, tm, D_MODEL), lambda i, j: (i, j, 0)),
            pl.BlockSpec((1, D_MODEL), lambda i, j: (0, 0)),
            pl.BlockSpec((3 * D_MODEL, D_MODEL), lambda i, j: (0, 0)),
            pl.BlockSpec((HEAD_DIM, tm), lambda i, j: (0, 0)),
            pl.BlockSpec((HEAD_DIM, tm), lambda i, j: (0, 0)),
        ],
        out_specs=[
            pl.BlockSpec((1, HEAD_PAIRS, PAIR_W, tm), lambda i, j: (i, 0, 0, j)),
            pl.BlockSpec((1, HEAD_PAIRS, tm, PAIR_W), lambda i, j: (i, 0, j, 0)),
            pl.BlockSpec((1, HEAD_PAIRS, PAIR_W, tm), lambda i, j: (i, 0, 0, j)),
        ],
        out_shape=[
            jax.ShapeDtypeStruct((b, HEAD_PAIRS, PAIR_W, l), BF16),
            jax.ShapeDtypeStruct((b, HEAD_PAIRS, l, PAIR_W), BF16),
            jax.ShapeDtypeStruct((b, HEAD_PAIRS, PAIR_W, l), BF16),
        ],
        compiler_params=_cparams(("parallel", "parallel")),
        name="qkv_proj",
    )(x, g, w_t, gq, gk)


def _attn_kernel(q_ref, kp_ref, kc_ref, kn_ref, vp_ref, vc_ref, vn_ref, hm_ref, bias_ref, rm_ref, o_ref):
    rm = rm_ref[0]
    ones = jnp.ones((ONES_ROWS, QB), BF16)

    def pair_body(j, carry):
        q2 = q_ref[0, j]
        kp, kc, kn = kp_ref[0, j], kc_ref[0, j], kn_ref[0, j]
        vp, vc, vn = vp_ref[0, j], vc_ref[0, j], vn_ref[0, j]
        for e in range(2):
            qm = q2 * hm_ref[e]
            s = jnp.concatenate([_dot(kp, qm), _dot(kc, qm), _dot(kn, qm)], axis=0)
            s = s + bias_ref[j, e] + rm
            m = jnp.max(s, axis=0, keepdims=True)
            pb = jnp.exp2(s - m).astype(BF16)
            lo, hi = HEAD_DIM * e, HEAD_DIM * (e + 1)
            ext = lambda v: jnp.concatenate([v[lo:hi], ones], axis=0)
            o = (_dot(ext(vp), pb[0:QB]) + _dot(ext(vc), pb[QB:2 * QB])
                 + _dot(ext(vn), pb[2 * QB:3 * QB]))
            o_ref[0, j, lo:hi, :] = (o[:HEAD_DIM] / o[HEAD_DIM:HEAD_DIM + 1]).astype(BF16)
        return carry

    lax.fori_loop(0, HEAD_PAIRS, pair_body, 0, unroll=2)


def _attn_call(q_t, k, v_t, hm, bias, rmask):
    b, _, _, l = q_t.shape
    nblk = l // QB
    prev = lambda j: jnp.maximum(j - 1, 0)
    nxt = lambda j: jnp.minimum(j + 1, nblk - 1)
    variant = lambda j: (j > 0).astype(jnp.int32) + (j == nblk - 1).astype(jnp.int32)
    kspec = lambda f: pl.BlockSpec((1, HEAD_PAIRS, QB, PAIR_W), lambda i, j: (i, 0, f(j), 0))
    vspec = lambda f: pl.BlockSpec((1, HEAD_PAIRS, PAIR_W, QB), lambda i, j: (i, 0, 0, f(j)))
    same = lambda j: j
    return pl.pallas_call(
        _attn_kernel,
        grid=(b, nblk),
        in_specs=[
            vspec(same),
            kspec(prev), kspec(same), kspec(nxt),
            vspec(prev), vspec(same), vspec(nxt),
            pl.BlockSpec((2, PAIR_W, QB), lambda i, j: (0, 0, 0)),
            pl.BlockSpec((HEAD_PAIRS, 2, KB, QB), lambda i, j: (0, 0, 0, 0), pipeline_mode=pl.Buffered(1)),
            pl.BlockSpec((1, KB, QB), lambda i, j: (variant(j), 0, 0)),
        ],
        out_specs=vspec(same),
        out_shape=jax.ShapeDtypeStruct((b, HEAD_PAIRS, PAIR_W, l), BF16),
        compiler_params=_cparams(("parallel", "parallel")),
        name="nbr_attention",
    )(q_t, k, k, k, v_t, v_t, v_t, hm, bias, rmask)


def _attn_tables(rpb):
    kr = np.arange(3 * Q_ROWS)[:, None, None, None]
    kc = np.arange(GRID_W)[None, :, None, None]
    r = np.arange(Q_ROWS)[None, None, :, None]
    c = np.arange(GRID_W)[None, None, None, :]
    shape = (3 * Q_ROWS, GRID_W, Q_ROWS, GRID_W)
    cs = np.clip(c - WIN_COLS // 2, 0, GRID_W - WIN_COLS)
    col_ok = ((kc >= cs) & (kc < cs + WIN_COLS))[0, :, 0, :]
    dc = (kc - c + WIN_COLS - 1)[0, :, 0, :]
    onehot = ((dc[None] == np.arange(2 * WIN_COLS - 1)[:, None, None]) & col_ok[None]).astype(np.float32)
    toep = jnp.einsum('hrd,dkc->hrkc', rpb.astype(F32), onehot, precision=lax.Precision.HIGHEST)
    toep = jnp.where(col_ok[None, None], toep * LOG2E, NEG)
    off = WIN_ROWS - 1 - Q_ROWS
    bias = jnp.stack([toep[:, off - rr:off - rr + 3 * Q_ROWS] for rr in range(Q_ROWS)], axis=3)
    bias = bias.reshape(HEAD_PAIRS, 2, KB, QB)
    first = np.broadcast_to(kr >= Q_ROWS, shape)
    inner = np.broadcast_to((kr >= r) & (kr < r + WIN_ROWS), shape)
    last = np.broadcast_to(kr < 2 * Q_ROWS, shape)
    rmask = np.where(np.stack([first, inner, last]), 0.0, NEG).astype(np.float32).reshape(3, KB, QB)
    return bias, jnp.asarray(rmask)


def _ffn_steps(c, x0_sc, hb_sc, acc_sc, wg_ref, wu_ref, wd_ref):
    hb = hb_sc[...]
    g = _dot(hb, wg_ref[...])
    u = _dot(hb, wu_ref[...])
    act = (g * (1.0 / (1.0 + jnp.exp(-g))) * u).astype(BF16)
    y = _dot(act, wd_ref[...])

    @pl.when(c == 0)
    def _():
        acc_sc[...] = y

    @pl.when(c != 0)
    def _():
        acc_sc[...] += y


def _phase_major(x):
    return jnp.swapaxes(x.reshape(-1, CHUNK, D_MODEL), 0, 1)


def _store_chunk_major(ref, x):
    xs = _phase_major(x)
    for s in range(CHUNK):
        ref[:, D_MODEL * s:D_MODEL * (s + 1)] = xs[s]


def _attn_ffn_kernel(x_ref, o_ref, wo_ref, gf_ref, wg_ref, wu_ref, wd_ref, gn_ref,
                     x1_ref, h1_ref, x0_sc, hb_sc, acc_sc):
    c = pl.program_id(2)

    @pl.when(c == 0)
    def _():
        o = o_ref[0].reshape(D_MODEL, -1)
        x0 = x_ref[0] + _dot_tn(o, wo_ref[...])
        x0_sc[...] = x0
        hb_sc[...] = _rms(x0, gf_ref[...]).astype(BF16)

    _ffn_steps(c, x0_sc, hb_sc, acc_sc, wg_ref, wu_ref, wd_ref)

    @pl.when(c == pl.num_programs(2) - 1)
    def _():
        x1 = x0_sc[...] + acc_sc[...]
        _store_chunk_major(x1_ref, x1)
        _store_chunk_major(h1_ref, _rms(x1, gn_ref[...]))


def _ffn_weight_specs(nlead):
    nff = D_FF // FF_CHUNK
    pick = lambda *idx: idx[nlead]
    return [
        pl.BlockSpec((D_MODEL, FF_CHUNK), lambda *idx: (0, pick(*idx))),
        pl.BlockSpec((D_MODEL, FF_CHUNK), lambda *idx: (0, pick(*idx) + nff)),
        pl.BlockSpec((FF_CHUNK, D_MODEL), lambda *idx: (pick(*idx), 0)),
    ]


def _attn_ffn_call(x, o_t, wo, gf, wgu, wd, gn, tm):
    b, l, _ = x.shape
    nff = D_FF // FF_CHUNK
    nt = l // tm
    row = pl.BlockSpec((1, tm, D_MODEL), lambda i, j, c: (i, j, 0))
    vec = pl.BlockSpec((1, D_MODEL), lambda i, j, c: (0, 0))
    out = pl.BlockSpec((tm // CHUNK, CHUNK * D_MODEL), lambda i, j, c: (i * nt + j, 0))
    return pl.pallas_call(
        _attn_ffn_kernel,
        grid=(b, nt, nff),
        in_specs=[
            row,
            pl.BlockSpec((1, HEAD_PAIRS, PAIR_W, tm), lambda i, j, c: (i, 0, 0, j)),
            pl.BlockSpec((D_MODEL, D_MODEL), lambda i, j, c: (0, 0)),
            vec,
            *_ffn_weight_specs(2),
            vec,
        ],
        out_specs=[out, out],
        out_shape=[jax.ShapeDtypeStruct((b * l // CHUNK, CHUNK * D_MODEL), F32)] * 2,
        scratch_shapes=[
            pltpu.VMEM((tm, D_MODEL), F32),
            pltpu.VMEM((tm, D_MODEL), BF16),
            pltpu.VMEM((tm, D_MODEL), F32),
        ],
        compiler_params=_cparams(("parallel", "parallel", "arbitrary")),
        name="attn_out_ffn",
    )(x, o_t, wo, gf, wgu, wgu, wd, gn)


def _glu_ffn_kernel(x_ref, z_ref, wglu_ref, gf_ref, wg_ref, wu_ref, wd_ref,
                    y_ref, x0_sc, hb_sc, acc_sc):
    c = pl.program_id(1)

    @pl.when(c == 0)
    def _():
        z = z_ref[...].reshape(-1, D_MODEL)
        x = jnp.concatenate([x_ref[:, D_MODEL * s:D_MODEL * (s + 1)] for s in range(CHUNK)], axis=0)
        zz = _dot(z, wglu_ref[...])
        za, zg = zz[:, :D_MODEL], zz[:, D_MODEL:]
        x0 = x + za * (1.0 / (1.0 + jnp.exp(-zg)))
        x0_sc[...] = x0
        hb_sc[...] = _rms(x0, gf_ref[...]).astype(BF16)

    _ffn_steps(c, x0_sc, hb_sc, acc_sc, wg_ref, wu_ref, wd_ref)

    @pl.when(c == pl.num_programs(1) - 1)
    def _():
        y = (x0_sc[...] + acc_sc[...]).reshape(CHUNK, -1, D_MODEL)
        y_ref[...] = jnp.swapaxes(y, 0, 1).reshape(-1, D_MODEL)


def _glu_ffn_call(x_ph, z_pm, wglu, gf, wgu, wd, tm):
    nc = x_ph.shape[0]
    nk = tm // CHUNK
    nff = D_FF // FF_CHUNK
    vec = pl.BlockSpec((1, D_MODEL), lambda i, c: (0, 0))
    return pl.pallas_call(
        _glu_ffn_kernel,
        grid=(nc // nk, nff),
        in_specs=[
            pl.BlockSpec((nk, CHUNK * D_MODEL), lambda i, c: (i, 0)),
            pl.BlockSpec((CHUNK, nk, D_MODEL), lambda i, c: (0, i, 0)),
            pl.BlockSpec((D_MODEL, 2 * D_MODEL), lambda i, c: (0, 0)),
            vec,
            *_ffn_weight_specs(1),
        ],
        out_specs=pl.BlockSpec((tm, D_MODEL), lambda i, c: (i, 0)),
        out_shape=jax.ShapeDtypeStruct((nc * CHUNK, D_MODEL), F32),
        scratch_shapes=[
            pltpu.VMEM((tm, D_MODEL), F32),
            pltpu.VMEM((tm, D_MODEL), BF16),
            pltpu.VMEM((tm, D_MODEL), F32),
        ],
        compiler_params=_cparams(("parallel", "arbitrary")),
        name="glu_out_ffn",
    )(x_ph, z_pm, wglu, gf, wgu, wgu, wd)


def _chunk_operands(h_refs, ut_sc, ht_sc):
    for s, h_ref in enumerate(h_refs):
        ht = h_ref[...].T
        if ht_sc is not None:
            ht_sc[s] = ht
        ut_sc[:, SSM_GROUP * s:SSM_GROUP * (s + 1), :] = (
            ht.reshape(GROUP_BLOCK, SSM_GROUP, -1).astype(BF16))


def _s5_state_kernel(*refs):
    h_refs = refs[:CHUNK]
    pt_ref, sf_ref, sb_ref, ut_sc, st_sc = refs[CHUNK:]
    _chunk_operands(h_refs, ut_sc, None)
    for g in range(GROUP_BLOCK):
        st_sc[g] = _dot(pt_ref[g], ut_sc[g]).T
    sf_ref[...] = jnp.swapaxes(st_sc[:, :, :SW], 0, 1)
    sb_ref[...] = jnp.swapaxes(st_sc[:, :, SW:], 0, 1)


def _h_specs(nk):
    return [pl.BlockSpec((nk, GB_W), lambda gb, i, s=s: (i, s * N_GB + gb)) for s in range(CHUNK)]


def _s5_state_call(h_ph, pt, nk):
    nc = h_ph.shape[0]
    sspec = pl.BlockSpec((nk, GROUP_BLOCK, SW), lambda gb, i: (i, gb, 0))
    return pl.pallas_call(
        _s5_state_kernel,
        grid=(N_GB, nc // nk),
        in_specs=[*_h_specs(nk),
                  pl.BlockSpec((GROUP_BLOCK, 2 * SW, CW), lambda gb, i: (gb, 0, 0))],
        out_specs=[sspec, sspec],
        out_shape=[jax.ShapeDtypeStruct((nc, N_GROUPS, SW), F32)] * 2,
        scratch_shapes=[pltpu.VMEM((GROUP_BLOCK, CW, nk), BF16),
                        pltpu.VMEM((GROUP_BLOCK, nk, 2 * SW), F32)],
        compiler_params=_cparams(("parallel", "parallel")),
        name="s5_chunk_states",
    )(*([h_ph] * CHUNK), pt)


def _scan_kernel(s_ref, ar_ref, ai_ref, xin_ref, x_sc, xs_sc, *, reverse):
    kc = s_ref.shape[1]

    @pl.when(pl.program_id(1) == 0)
    def _():
        x_sc[...] = jnp.zeros_like(x_sc)
        xs_sc[...] = jnp.zeros_like(xs_sc)

    ar = ar_ref[...]
    ai = ai_ref[...]

    def body(i, carry):
        x, xs = carry
        k = kc - 1 - i if reverse else i
        s = s_ref[0, k]
        xin_ref[0, k] = x.astype(BF16)
        ss = pltpu.roll(s, STATE, 1)
        return ar * x + ai * xs + s, ar * xs - ai * x + ss

    x, xs = lax.fori_loop(0, kc, body, (x_sc[...], xs_sc[...]), unroll=8)
    x_sc[...] = x
    xs_sc[...] = xs


def _scan_call(s, ar, ai, kc, reverse):
    b, ncs = s.shape[:2]
    nkb = ncs // kc
    blk = (lambda i, j: (i, nkb - 1 - j, 0, 0)) if reverse else (lambda i, j: (i, j, 0, 0))
    tab = pl.BlockSpec((N_GROUPS, SW), lambda i, j: (0, 0))
    return pl.pallas_call(
        functools.partial(_scan_kernel, reverse=reverse),
        grid=(b, nkb),
        in_specs=[pl.BlockSpec((1, kc, N_GROUPS, SW), blk), tab, tab],
        out_specs=pl.BlockSpec((1, kc, N_GROUPS, SW), blk),
        out_shape=jax.ShapeDtypeStruct(s.shape, BF16),
        scratch_shapes=[pltpu.VMEM((N_GROUPS, SW), F32)] * 2,
        compiler_params=_cparams(("parallel", "arbitrary")),
        name="s5_chunk_scan_bwd" if reverse else "s5_chunk_scan_fwd",
    )(s, ar, ai)


def _s5_out_kernel(*refs):
    h_refs = refs[:CHUNK]
    xf_ref, xb_ref, mt_ref, qt_ref, d_ref, z_ref, ut_sc, ht_sc, yt_sc = refs[CHUNK:]
    _chunk_operands(h_refs, ut_sc, ht_sc)
    xf = jnp.swapaxes(xf_ref[...].astype(F32), 0, 1)
    xb = jnp.swapaxes(xb_ref[...].astype(F32), 0, 1)
    for g in range(GROUP_BLOCK):
        xin = jnp.concatenate([xf[g], xb[g]], axis=1).astype(BF16)
        yt_sc[g] = _dot(mt_ref[g], ut_sc[g]) + _dot_nt(qt_ref[g], xin)
    d = d_ref[...]
    for t in range(CHUNK):
        y = yt_sc[:, SSM_GROUP * t:SSM_GROUP * (t + 1), :].reshape(GB_W, -1)
        v = y + d * ht_sc[t]
        z_ref[t] = jax.nn.gelu(v, approximate=True).T.astype(BF16)


def _s5_out_call(h_ph, xin_f, xin_b, mt, qt, d_col, nk):
    nc = h_ph.shape[0]
    xspec = pl.BlockSpec((nk, GROUP_BLOCK, SW), lambda gb, i: (i, gb, 0))
    wspec = pl.BlockSpec((GROUP_BLOCK, CW, CW), lambda gb, i: (gb, 0, 0))
    return pl.pallas_call(
        _s5_out_kernel,
        grid=(N_GB, nc // nk),
        in_specs=[*_h_specs(nk), xspec, xspec, wspec, wspec,
                  pl.BlockSpec((GB_W, nk), lambda gb, i: (gb, 0))],
        out_specs=pl.BlockSpec((CHUNK, nk, GB_W), lambda gb, i: (0, i, gb)),
        out_shape=jax.ShapeDtypeStruct((CHUNK, nc, D_MODEL), BF16),
        scratch_shapes=[
            pltpu.VMEM((GROUP_BLOCK, CW, nk), BF16),
            pltpu.VMEM((CHUNK, GB_W, nk), F32),
            pltpu.VMEM((GROUP_BLOCK, CW, nk), F32),
        ],
        compiler_params=_cparams(("parallel", "parallel")),
        name="s5_outputs",
    )(*([h_ph] * CHUNK), xin_f, xin_b, mt, qt, d_col)


def _s5_matrices(lam_re, lam_im, log_step, b_re, b_im, c_re, c_im):
    hp = lax.Precision.HIGHEST
    dt = jnp.exp(log_step.astype(F32))[..., None]
    lam_re = lam_re.astype(F32)
    lam_im = lam_im.astype(F32)
    mag = jnp.exp(lam_re * dt)
    ang = lam_im * dt
    lb_re = mag * jnp.cos(ang)
    lb_im = mag * jnp.sin(ang)
    den = lam_re * lam_re + lam_im * lam_im
    nr = lb_re - 1.0
    ni = lb_im
    coef_re = (nr * lam_re + ni * lam_im) / den
    coef_im = (ni * lam_re - nr * lam_im) / den
    b_re = b_re.astype(F32)
    b_im = b_im.astype(F32)
    bb_re = coef_re[..., None] * b_re - coef_im[..., None] * b_im
    bb_im = coef_re[..., None] * b_im + coef_im[..., None] * b_re
    c_re = c_re.astype(F32)
    c_im = c_im.astype(F32)

    pw_re = [jnp.ones_like(lb_re)]
    pw_im = [jnp.zeros_like(lb_im)]
    for _ in range(CHUNK):
        pr, pi = pw_re[-1], pw_im[-1]
        pw_re.append(pr * lb_re - pi * lb_im)
        pw_im.append(pr * lb_im + pi * lb_re)
    pw_re = jnp.stack(pw_re)
    pw_im = jnp.stack(pw_im)

    w_re = pw_re[:CHUNK, ..., None] * bb_re - pw_im[:CHUNK, ..., None] * bb_im
    w_im = pw_re[:CHUNK, ..., None] * bb_im + pw_im[:CHUNK, ..., None] * bb_re
    kern = (jnp.einsum('dgcn,tdgnk->tdgck', c_re, w_re, precision=hp)
            - jnp.einsum('dgcn,tdgnk->tdgck', c_im, w_im, precision=hp))

    s_i = np.arange(CHUNK)[:, None]
    t_i = np.arange(CHUNK)[None, :]
    kf = jnp.where((t_i >= s_i)[..., None, None, None], kern[np.clip(t_i - s_i, 0, CHUNK - 1), 0], 0.0)
    kb = jnp.where((s_i >= t_i)[..., None, None, None], kern[np.clip(s_i - t_i, 0, CHUNK - 1), 1], 0.0)
    m = kf + kb
    mt = m.transpose(2, 1, 3, 0, 4).reshape(N_GROUPS, CW, CW)

    rev = np.arange(CHUNK)[::-1]
    pf = jnp.stack([w_re[rev, 0], w_im[rev, 0]])
    pb = jnp.stack([w_re[:, 1], w_im[:, 1]])
    p_all = jnp.stack([pf, pb])
    pt = p_all.transpose(3, 0, 1, 4, 2, 5).reshape(N_GROUPS, 2 * SW, CW)

    def carry(d, taus):
        cr = c_re[d][None]
        ci = c_im[d][None]
        pr = pw_re[taus, d][:, :, None, :]
        pi = pw_im[taus, d][:, :, None, :]
        return jnp.stack([cr * pr - ci * pi, -(cr * pi + ci * pr)])

    q_all = jnp.stack([carry(0, np.arange(1, CHUNK + 1)), carry(1, CHUNK - np.arange(CHUNK))])
    qt = q_all.transpose(3, 2, 4, 0, 1, 5).reshape(N_GROUPS, CW, 2 * SW)

    a_re = pw_re[CHUNK]
    a_im = pw_im[CHUNK]
    ar = jnp.concatenate([a_re, a_re], axis=-1)
    ai = jnp.concatenate([-a_im, a_im], axis=-1)
    return mt.astype(BF16), pt.astype(BF16), qt.astype(BF16), ar, ai


def _trunk(x, p, tm, nk):
    b, l, _ = x.shape
    nc = b * l // CHUNK
    kc = min(256, l // CHUNK)
    q_t, k, v_t = _qkv_call(x, p['g_mix0'], p['wqkv_t'], p['gq'], p['gk'], tm)
    o_t = _attn_call(q_t, k, v_t, p['hm'], p['bias'], p['rmask'])
    x_ph, h_ph = _attn_ffn_call(x, o_t, p['wo'], p['g_ffn0'], p['wgu0'], p['wd0'], p['g_mix1'], tm)

    sf, sb = _s5_state_call(h_ph, p['pt'], nk)
    sshape = (b, l // CHUNK, N_GROUPS, SW)
    xin_f = _scan_call(sf.reshape(sshape), p['ar'][0], p['ai'][0], kc, False)
    xin_b = _scan_call(sb.reshape(sshape), p['ar'][1], p['ai'][1], kc, True)
    z_pm = _s5_out_call(h_ph, xin_f.reshape(nc, N_GROUPS, SW), xin_b.reshape(nc, N_GROUPS, SW),
                        p['mt'], p['qt'], p['d_col'], nk)
    y = _glu_ffn_call(x_ph, z_pm, p['wglu'], p['g_ffn1'], p['wgu1'], p['wd1'], tm)
    return y.reshape(b, l, D_MODEL)


def kernel(x_prompt, x_sample, norm_mix, norm_ffn, w_qkv, w_o, q_gain, k_gain, rpb, lam_re, lam_im, log_step,
           b_re, b_im, c_re, c_im, d_skip, w_glu, w_gate_up, w_down):
    tm, nk = 512, 128
    scale = HEAD_DIM ** -0.5 * LOG2E
    mt, pt, qt, ar, ai = _s5_matrices(lam_re[0], lam_im[0], log_step[0], b_re[0], b_im[0], c_re[0], c_im[0])
    bias, rmask = _attn_tables(rpb[0])
    rows = np.arange(PAIR_W)[None, :, None]
    hm = np.broadcast_to((rows // HEAD_DIM) == np.arange(2)[:, None, None], (2, PAIR_W, QB))
    p = dict(
        g_mix0=norm_mix[0].astype(F32)[None], g_mix1=norm_mix[1].astype(F32)[None],
        g_ffn0=norm_ffn[0].astype(F32)[None], g_ffn1=norm_ffn[1].astype(F32)[None],
        wqkv_t=w_qkv[0].T.astype(BF16), wo=w_o[0].astype(BF16),
        gq=jnp.broadcast_to((q_gain[0].astype(F32) * scale)[:, None], (HEAD_DIM, tm)),
        gk=jnp.broadcast_to(k_gain[0].astype(F32)[:, None], (HEAD_DIM, tm)),
        hm=jnp.asarray(hm, BF16), bias=bias, rmask=rmask,
        wgu0=w_gate_up[0].astype(BF16), wd0=w_down[0].astype(BF16),
        wgu1=w_gate_up[1].astype(BF16), wd1=w_down[1].astype(BF16),
        wglu=w_glu[0].astype(BF16),
        mt=mt, pt=pt, qt=qt, ar=ar, ai=ai,
        d_col=jnp.broadcast_to(d_skip[0].astype(F32)[:, None], (D_MODEL, nk)),
    )
    return _trunk(x_prompt, p, tm, nk), _trunk(x_sample, p, tm, nk)
```

```python
import functools

import numpy as np
import jax
import jax.numpy as jnp
from jax import lax
from jax.experimental import pallas as pl
from jax.experimental.pallas import tpu as pltpu

D_MODEL = 1024
GRID_W = 64
N_HEADS = 16
HEAD_DIM = D_MODEL // N_HEADS
WIN_ROWS = 8
WIN_COLS = 16
SSM_GROUP = 16
N_GROUPS = D_MODEL // SSM_GROUP
STATE = 64
D_FF = 2816
EPS = 1e-6

F32 = jnp.float32
BF16 = jnp.bfloat16

V7X_LANES = 128
V7X_VMEM_BYTES = 64 * 1024 * 1024
VMEM_LIMIT = (V7X_VMEM_BYTES * 7) // 8

HEAD_PAIRS = N_HEADS // 2
PAIR_W = 2 * HEAD_DIM
Q_ROWS = 4
QB = Q_ROWS * GRID_W
KB = 3 * QB
NEG = -1e30
ONES_ROWS = 16
LOG2E = 1.4426950408889634
RM_ROWS = 16

CHUNK = 16
CW = CHUNK * SSM_GROUP
SW = 2 * STATE
GROUP_BLOCK = 16
GB_W = GROUP_BLOCK * SSM_GROUP
N_GB = N_GROUPS // GROUP_BLOCK
FF_CHUNK = D_FF // 2


def _cparams(sem):
    return pltpu.CompilerParams(dimension_semantics=sem, vmem_limit_bytes=VMEM_LIMIT)


def _rms(x, g):
    return x * lax.rsqrt(jnp.mean(x * x, axis=-1, keepdims=True) + EPS) * g


def _dot(a, b):
    return jnp.dot(a, b, preferred_element_type=F32)


def _dot_nt(a, b):
    return lax.dot_general(a, b, (((1,), (1,)), ((), ())), preferred_element_type=F32)


def _dot_tn(a, b):
    return lax.dot_general(a, b, (((0,), (0,)), ((), ())), preferred_element_type=F32)


def _qkv_kernel(x_ref, g_ref, w_ref, gq_ref, gk_ref, q_ref, k_ref, v_ref):
    x = x_ref[0]
    tm = x.shape[0]
    h = _rms(x, g_ref[...]).astype(BF16)
    qkv = _dot_nt(w_ref[...], h)

    def head_norm(t, gain):
        t3 = t.reshape(N_HEADS, HEAD_DIM, tm)
        y = t3 * lax.rsqrt(jnp.mean(t3 * t3, axis=1, keepdims=True) + EPS)
        return y * gain[None]

    q = head_norm(qkv[0:D_MODEL], gq_ref[...])
    k = head_norm(qkv[D_MODEL:2 * D_MODEL], gk_ref[...])
    v = qkv[2 * D_MODEL:3 * D_MODEL]
    q_ref[0] = q.reshape(HEAD_PAIRS, PAIR_W, tm).astype(BF16)
    v_ref[0] = v.reshape(HEAD_PAIRS, PAIR_W, tm).astype(BF16)
    kt = k.reshape(D_MODEL, tm).T
    for j in range(HEAD_PAIRS):
        k_ref[0, j] = kt[:, PAIR_W * j:PAIR_W * (j + 1)].astype(BF16)


def _qkv_call(x, g, w_t, gq, gk, tm):
    b, l, _ = x.shape
    return pl.pallas_call(
        _qkv_kernel,
        grid=(b, l // tm),
        in_specs=[
            pl.BlockSpec((1, tm, D_MODEL), lambda i, j: (i, j, 0)),
            pl.BlockSpec((1, D_MODEL), lambda i, j: (0, 0)),
            pl.BlockSpec((3 * D_MODEL, D_MODEL), lambda i, j: (0, 0)),
            pl.BlockSpec((HEAD_DIM, tm), lambda i, j: (0, 0)),
            pl.BlockSpec((HEAD_DIM, tm), lambda i, j: (0, 0)),
        ],
        out_specs=[
            pl.BlockSpec((1, HEAD_PAIRS, PAIR_W, tm), lambda i, j: (i, 0, 0, j)),
            pl.BlockSpec((1, HEAD_PAIRS, tm, PAIR_W), lambda i, j: (i, 0, j, 0)),
            pl.BlockSpec((1, HEAD_PAIRS, PAIR_W, tm), lambda i, j: (i, 0, 0, j)),
        ],
        out_shape=[
            jax.ShapeDtypeStruct((b, HEAD_PAIRS, PAIR_W, l), BF16),
            jax.ShapeDtypeStruct((b, HEAD_PAIRS, l, PAIR_W), BF16),
            jax.ShapeDtypeStruct((b, HEAD_PAIRS, PAIR_W, l), BF16),
        ],
        compiler_params=_cparams(("parallel", "parallel")),
        name="qkv_proj",
    )(x, g, w_t, gq, gk)


def _attn_kernel(q_ref, kp_ref, kc_ref, kn_ref, vp_ref, vc_ref, vn_ref, hm_ref, bias_ref, rm_ref, o_ref,
                 sa_sc, sb_sc, ma_sc, mb_sc):
    ones = jnp.ones((ONES_ROWS, QB), BF16)

    def scores(j, s_sc, m_sc):
        q2 = q_ref[0, j]
        for e in range(2):
            qm = q2 * hm_ref[e]
            mx = None
            for blk, k_ref in enumerate((kp_ref, kc_ref, kn_ref)):
                rows = slice(blk * QB, (blk + 1) * QB)
                s = _dot(k_ref[0, j], qm) + bias_ref[j, e, rows, :]
                s = s.reshape(Q_ROWS, GRID_W, QB) + rm_ref[0, Q_ROWS * blk:Q_ROWS * (blk + 1), :][:, None, :]
                s_sc[e, rows, :] = s.reshape(QB, QB)
                pm = jnp.max(s, axis=0)
                mx = pm if mx is None else jnp.maximum(mx, pm)
            m_sc[e] = mx

    def finish(j, s_sc, m_sc):
        for e in range(2):
            lo, hi = HEAD_DIM * e, HEAD_DIM * (e + 1)
            m = jnp.max(m_sc[e], axis=0, keepdims=True)
            pb = jnp.exp2(s_sc[e] - m).astype(BF16)
            o = None
            for blk, v_ref in enumerate((vp_ref, vc_ref, vn_ref)):
                v = jnp.concatenate([v_ref[0, j, lo:hi, :], ones], axis=0)
                ob = _dot(v, pb[blk * QB:(blk + 1) * QB])
                o = ob if o is None else o + ob
            o_ref[0, j, lo:hi, :] = (o[:HEAD_DIM] / o[HEAD_DIM:HEAD_DIM + 1]).astype(BF16)

    scores(0, sa_sc, ma_sc)

    def body(i, carry):
        j = 2 * i
        scores(j + 1, sb_sc, mb_sc)
        finish(j, sa_sc, ma_sc)
        scores(jnp.minimum(j + 2, HEAD_PAIRS - 1), sa_sc, ma_sc)
        finish(j + 1, sb_sc, mb_sc)
        return carry

    lax.fori_loop(0, HEAD_PAIRS // 2, body, 0)


def _attn_call(q_t, k, v_t, hm, bias, rmask):
    b, _, _, l = q_t.shape
    nblk = l // QB
    prev = lambda j: jnp.maximum(j - 1, 0)
    nxt = lambda j: jnp.minimum(j + 1, nblk - 1)
    variant = lambda j: (j > 0).astype(jnp.int32) + (j == nblk - 1).astype(jnp.int32)
    kspec = lambda f: pl.BlockSpec((1, HEAD_PAIRS, QB, PAIR_W), lambda i, j: (i, 0, f(j), 0))
    vspec = lambda f: pl.BlockSpec((1, HEAD_PAIRS, PAIR_W, QB), lambda i, j: (i, 0, 0, f(j)))
    same = lambda j: j
    return pl.pallas_call(
        _attn_kernel,
        grid=(b, nblk),
        in_specs=[
            vspec(same),
            kspec(prev), kspec(same), kspec(nxt),
            vspec(prev), vspec(same), vspec(nxt),
            pl.BlockSpec((2, PAIR_W, QB), lambda i, j: (0, 0, 0)),
            pl.BlockSpec((HEAD_PAIRS, 2, KB, QB), lambda i, j: (0, 0, 0, 0), pipeline_mode=pl.Buffered(1)),
            pl.BlockSpec((1, RM_ROWS, QB), lambda i, j: (variant(j), 0, 0)),
        ],
        out_specs=vspec(same),
        out_shape=jax.ShapeDtypeStruct((b, HEAD_PAIRS, PAIR_W, l), BF16),
        scratch_shapes=[pltpu.VMEM((2, KB, QB), F32)] * 2 + [pltpu.VMEM((2, GRID_W, QB), F32)] * 2,
        compiler_params=_cparams(("parallel", "parallel")),
        name="nbr_attention",
    )(q_t, k, k, k, v_t, v_t, v_t, hm, bias, rmask)


def _attn_tables(rpb):
    kr = np.arange(3 * Q_ROWS)[:, None, None, None]
    kc = np.arange(GRID_W)[None, :, None, None]
    r = np.arange(Q_ROWS)[None, None, :, None]
    c = np.arange(GRID_W)[None, None, None, :]
    shape = (3 * Q_ROWS, GRID_W, Q_ROWS, GRID_W)
    cs = np.clip(c - WIN_COLS // 2, 0, GRID_W - WIN_COLS)
    col_ok = ((kc >= cs) & (kc < cs + WIN_COLS))[0, :, 0, :]
    dc = (kc - c + WIN_COLS - 1)[0, :, 0, :]
    onehot = ((dc[None] == np.arange(2 * WIN_COLS - 1)[:, None, None]) & col_ok[None]).astype(np.float32)
    toep = jnp.einsum('hrd,dkc->hrkc', rpb.astype(F32), onehot, precision=lax.Precision.HIGHEST)
    toep = jnp.where(col_ok[None, None], toep * LOG2E, NEG)
    off = WIN_ROWS - 1 - Q_ROWS
    bias = jnp.concatenate([toep[:, off - rr:off - rr + 3 * Q_ROWS] for rr in range(Q_ROWS)], axis=-1)
    bias = bias.reshape(HEAD_PAIRS, 2, KB, QB)
    rshape = (3 * Q_ROWS, 1, Q_ROWS, GRID_W)
    first = np.broadcast_to(kr >= Q_ROWS, rshape)
    inner = np.broadcast_to((kr >= r) & (kr < r + WIN_ROWS), rshape)
    last = np.broadcast_to(kr < 2 * Q_ROWS, rshape)
    rmask = np.where(np.stack([first, inner, last]), 0.0, NEG).astype(np.float32).reshape(3, 3 * Q_ROWS, QB)
    rmask = np.pad(rmask, ((0, 0), (0, RM_ROWS - 3 * Q_ROWS), (0, 0)))
    return bias, jnp.asarray(rmask)


def _ffn(x0, gf_ref, wgu_ref, wd_ref):
    hb = _rms(x0, gf_ref[...]).astype(BF16)
    y = x0
    for c in range(D_FF // FF_CHUNK):
        g = _dot(hb, wgu_ref[:, FF_CHUNK * c:FF_CHUNK * (c + 1)])
        u = _dot(hb, wgu_ref[:, D_FF + FF_CHUNK * c:D_FF + FF_CHUNK * (c + 1)])
        act = (g * (1.0 / (1.0 + jnp.exp(-g))) * u).astype(BF16)
        y = y + _dot(act, wd_ref[FF_CHUNK * c:FF_CHUNK * (c + 1), :])
    return y


def _resident(shape):
    return pl.BlockSpec(shape, lambda *idx: (0,) * len(shape), pipeline_mode=pl.Buffered(1))


def _phase_major(x):
    return jnp.swapaxes(x.reshape(-1, CHUNK, D_MODEL), 0, 1)


def _store_chunk_major(ref, x):
    xs = _phase_major(x)
    for s in range(CHUNK):
        ref[:, D_MODEL * s:D_MODEL * (s + 1)] = xs[s]


def _attn_ffn_kernel(x_ref, o_ref, wo_ref, gf_ref, wgu_ref, wd_ref, gn_ref, x1_ref, h1_ref):
    o = o_ref[0].reshape(D_MODEL, -1)
    x0 = x_ref[0] + _dot_tn(o, wo_ref[...])
    x1 = _ffn(x0, gf_ref, wgu_ref, wd_ref)
    _store_chunk_major(x1_ref, x1)
    _store_chunk_major(h1_ref, _rms(x1, gn_ref[...]))


def _attn_ffn_call(x, o_t, wo, gf, wgu, wd, gn, tm):
    b, l, _ = x.shape
    nt = l // tm
    out = pl.BlockSpec((tm // CHUNK, CHUNK * D_MODEL), lambda i, j: (i * nt + j, 0))
    return pl.pallas_call(
        _attn_ffn_kernel,
        grid=(b, nt),
        in_specs=[
            pl.BlockSpec((1, tm, D_MODEL), lambda i, j: (i, j, 0)),
            pl.BlockSpec((1, HEAD_PAIRS, PAIR_W, tm), lambda i, j: (i, 0, 0, j)),
            _resident((D_MODEL, D_MODEL)),
            _resident((1, D_MODEL)),
            _resident((D_MODEL, 2 * D_FF)),
            _resident((D_FF, D_MODEL)),
            _resident((1, D_MODEL)),
        ],
        out_specs=[out, out],
        out_shape=[jax.ShapeDtypeStruct((b * l // CHUNK, CHUNK * D_MODEL), F32)] * 2,
        compiler_params=_cparams(("parallel", "parallel")),
        name="attn_out_ffn",
    )(x, o_t, wo, gf, wgu, wd, gn)


def _glu_ffn_kernel(x_ref, z_ref, wglu_ref, gf_ref, wgu_ref, wd_ref, y_ref):
    z = z_ref[...].reshape(-1, D_MODEL)
    x = jnp.concatenate([x_ref[:, D_MODEL * s:D_MODEL * (s + 1)] for s in range(CHUNK)], axis=0)
    zz = _dot(z, wglu_ref[...])
    za, zg = zz[:, :D_MODEL], zz[:, D_MODEL:]
    x0 = x + za * (1.0 / (1.0 + jnp.exp(-zg)))
    y = _ffn(x0, gf_ref, wgu_ref, wd_ref).reshape(CHUNK, -1, D_MODEL)
    y_ref[...] = jnp.swapaxes(y, 0, 1).reshape(-1, D_MODEL)


def _glu_ffn_call(x_ph, z_pm, wglu, gf, wgu, wd, tm):
    nc = x_ph.shape[0]
    nk = tm // CHUNK
    return pl.pallas_call(
        _glu_ffn_kernel,
        grid=(nc // nk,),
        in_specs=[
            pl.BlockSpec((nk, CHUNK * D_MODEL), lambda i: (i, 0)),
            pl.BlockSpec((CHUNK, nk, D_MODEL), lambda i: (0, i, 0)),
            _resident((D_MODEL, 2 * D_MODEL)),
            _resident((1, D_MODEL)),
            _resident((D_MODEL, 2 * D_FF)),
            _resident((D_FF, D_MODEL)),
        ],
        out_specs=pl.BlockSpec((tm, D_MODEL), lambda i: (i, 0)),
        out_shape=jax.ShapeDtypeStruct((nc * CHUNK, D_MODEL), F32),
        compiler_params=_cparams(("parallel",)),
        name="glu_out_ffn",
    )(x_ph, z_pm, wglu, gf, wgu, wd)


def _chunk_operands(h_refs, ut_sc, ht_sc):
    for s, h_ref in enumerate(h_refs):
        ht = h_ref[...].T
        if ht_sc is not None:
            ht_sc[s] = ht
        ut_sc[:, SSM_GROUP * s:SSM_GROUP * (s + 1), :] = (
            ht.reshape(GROUP_BLOCK, SSM_GROUP, -1).astype(BF16))


def _s5_state_kernel(*refs):
    h_refs = refs[:CHUNK]
    pt_ref, sf_ref, sb_ref, ut_sc, st_sc = refs[CHUNK:]
    _chunk_operands(h_refs, ut_sc, None)
    for g in range(GROUP_BLOCK):
        st_sc[g] = _dot(pt_ref[g], ut_sc[g]).T
    sf_ref[...] = jnp.swapaxes(st_sc[:, :, :SW], 0, 1)
    sb_ref[...] = jnp.swapaxes(st_sc[:, :, SW:], 0, 1)


def _h_specs(nk):
    return [pl.BlockSpec((nk, GB_W), lambda gb, i, s=s: (i, s * N_GB + gb)) for s in range(CHUNK)]


def _s5_state_call(h_ph, pt, nk):
    nc = h_ph.shape[0]
    sspec = pl.BlockSpec((nk, GROUP_BLOCK, SW), lambda gb, i: (i, gb, 0))
    return pl.pallas_call(
        _s5_state_kernel,
        grid=(N_GB, nc // nk),
        in_specs=[*_h_specs(nk),
                  pl.BlockSpec((GROUP_BLOCK, 2 * SW, CW), lambda gb, i: (gb, 0, 0))],
        out_specs=[sspec, sspec],
        out_shape=[jax.ShapeDtypeStruct((nc, N_GROUPS, SW), F32)] * 2,
        scratch_shapes=[pltpu.VMEM((GROUP_BLOCK, CW, nk), BF16),
                        pltpu.VMEM((GROUP_BLOCK, nk, 2 * SW), F32)],
        compiler_params=_cparams(("parallel", "parallel")),
        name="s5_chunk_states",
    )(*([h_ph] * CHUNK), pt)


def _scan_kernel(s_ref, ar_ref, ai_ref, xin_ref, x_sc, xs_sc, *, reverse):
    kc = s_ref.shape[1]

    @pl.when(pl.program_id(1) == 0)
    def _():
        x_sc[...] = jnp.zeros_like(x_sc)
        xs_sc[...] = jnp.zeros_like(xs_sc)

    ar = ar_ref[...]
    ai = ai_ref[...]

    def body(i, carry):
        x, xs = carry
        k = kc - 1 - i if reverse else i
        s = s_ref[0, k]
        xin_ref[0, k] = x.astype(BF16)
        ss = pltpu.roll(s, STATE, 1)
        return ar * x + ai * xs + s, ar * xs - ai * x + ss

    x, xs = lax.fori_loop(0, kc, body, (x_sc[...], xs_sc[...]), unroll=8)
    x_sc[...] = x
    xs_sc[...] = xs


def _scan_call(s, ar, ai, kc, reverse):
    b, ncs = s.shape[:2]
    nkb = ncs // kc
    blk = (lambda i, j: (i, nkb - 1 - j, 0, 0)) if reverse else (lambda i, j: (i, j, 0, 0))
    tab = pl.BlockSpec((N_GROUPS, SW), lambda i, j: (0, 0))
    return pl.pallas_call(
        functools.partial(_scan_kernel, reverse=reverse),
        grid=(b, nkb),
        in_specs=[pl.BlockSpec((1, kc, N_GROUPS, SW), blk), tab, tab],
        out_specs=pl.BlockSpec((1, kc, N_GROUPS, SW), blk),
        out_shape=jax.ShapeDtypeStruct(s.shape, BF16),
        scratch_shapes=[pltpu.VMEM((N_GROUPS, SW), F32)] * 2,
        compiler_params=_cparams(("parallel", "arbitrary")),
        name="s5_chunk_scan_bwd" if reverse else "s5_chunk_scan_fwd",
    )(s, ar, ai)


def _s5_out_kernel(*refs):
    h_refs = refs[:CHUNK]
    xf_ref, xb_ref, mt_ref, qt_ref, d_ref, z_ref, ut_sc, ht_sc, yt_sc = refs[CHUNK:]
    _chunk_operands(h_refs, ut_sc, ht_sc)
    xf = jnp.swapaxes(xf_ref[...].astype(F32), 0, 1)
    xb = jnp.swapaxes(xb_ref[...].astype(F32), 0, 1)
    for g in range(GROUP_BLOCK):
        xin = jnp.concatenate([xf[g], xb[g]], axis=1).astype(BF16)
        yt_sc[g] = _dot(mt_ref[g], ut_sc[g]) + _dot_nt(qt_ref[g], xin)
    d = d_ref[...]
    for t in range(CHUNK):
        y = yt_sc[:, SSM_GROUP * t:SSM_GROUP * (t + 1), :].reshape(GB_W, -1)
        v = y + d * ht_sc[t]
        z_ref[t] = jax.nn.gelu(v, approximate=True).T.astype(BF16)


def _s5_out_call(h_ph, xin_f, xin_b, mt, qt, d_col, nk):
    nc = h_ph.shape[0]
    xspec = pl.BlockSpec((nk, GROUP_BLOCK, SW), lambda gb, i: (i, gb, 0))
    wspec = pl.BlockSpec((GROUP_BLOCK, CW, CW), lambda gb, i: (gb, 0, 0))
    return pl.pallas_call(
        _s5_out_kernel,
        grid=(N_GB, nc // nk),
        in_specs=[*_h_specs(nk), xspec, xspec, wspec, wspec,
                  pl.BlockSpec((GB_W, nk), lambda gb, i: (gb, 0))],
        out_specs=pl.BlockSpec((CHUNK, nk, GB_W), lambda gb, i: (0, i, gb)),
        out_shape=jax.ShapeDtypeStruct((CHUNK, nc, D_MODEL), BF16),
        scratch_shapes=[
            pltpu.VMEM((GROUP_BLOCK, CW, nk), BF16),
            pltpu.VMEM((CHUNK, GB_W, nk), F32),
            pltpu.VMEM((GROUP_BLOCK, CW, nk), F32),
        ],
        compiler_params=_cparams(("parallel", "parallel")),
        name="s5_outputs",
    )(*([h_ph] * CHUNK), xin_f, xin_b, mt, qt, d_col)


def _s5_matrices(lam_re, lam_im, log_step, b_re, b_im, c_re, c_im):
    hp = lax.Precision.HIGHEST
    dt = jnp.exp(log_step.astype(F32))[..., None]
    lam_re = lam_re.astype(F32)
    lam_im = lam_im.astype(F32)
    mag = jnp.exp(lam_re * dt)
    ang = lam_im * dt
    lb_re = mag * jnp.cos(ang)
    lb_im = mag * jnp.sin(ang)
    den = lam_re * lam_re + lam_im * lam_im
    nr = lb_re - 1.0
    ni = lb_im
    coef_re = (nr * lam_re + ni * lam_im) / den
    coef_im = (ni * lam_re - nr * lam_im) / den
    b_re = b_re.astype(F32)
    b_im = b_im.astype(F32)
    bb_re = coef_re[..., None] * b_re - coef_im[..., None] * b_im
    bb_im = coef_re[..., None] * b_im + coef_im[..., None] * b_re
    c_re = c_re.astype(F32)
    c_im = c_im.astype(F32)

    pw_re = [jnp.ones_like(lb_re)]
    pw_im = [jnp.zeros_like(lb_im)]
    for _ in range(CHUNK):
        pr, pi = pw_re[-1], pw_im[-1]
        pw_re.append(pr * lb_re - pi * lb_im)
        pw_im.append(pr * lb_im + pi * lb_re)
    pw_re = jnp.stack(pw_re)
    pw_im = jnp.stack(pw_im)

    w_re = pw_re[:CHUNK, ..., None] * bb_re - pw_im[:CHUNK, ..., None] * bb_im
    w_im = pw_re[:CHUNK, ..., None] * bb_im + pw_im[:CHUNK, ..., None] * bb_re
    kern = (jnp.einsum('dgcn,tdgnk->tdgck', c_re, w_re, precision=hp)
            - jnp.einsum('dgcn,tdgnk->tdgck', c_im, w_im, precision=hp))

    s_i = np.arange(CHUNK)[:, None]
    t_i = np.arange(CHUNK)[None, :]
    kf = jnp.where((t_i >= s_i)[..., None, None, None], kern[np.clip(t_i - s_i, 0, CHUNK - 1), 0], 0.0)
    kb = jnp.where((s_i >= t_i)[..., None, None, None], kern[np.clip(s_i - t_i, 0, CHUNK - 1), 1], 0.0)
    m = kf + kb
    mt = m.transpose(2, 1, 3, 0, 4).reshape(N_GROUPS, CW, CW)

    rev = np.arange(CHUNK)[::-1]
    pf = jnp.stack([w_re[rev, 0], w_im[rev, 0]])
    pb = jnp.stack([w_re[:, 1], w_im[:, 1]])
    p_all = jnp.stack([pf, pb])
    pt = p_all.transpose(3, 0, 1, 4, 2, 5).reshape(N_GROUPS, 2 * SW, CW)

    def carry(d, taus):
        cr = c_re[d][None]
        ci = c_im[d][None]
        pr = pw_re[taus, d][:, :, None, :]
        pi = pw_im[taus, d][:, :, None, :]
        return jnp.stack([cr * pr - ci * pi, -(cr * pi + ci * pr)])

    q_all = jnp.stack([carry(0, np.arange(1, CHUNK + 1)), carry(1, CHUNK - np.arange(CHUNK))])
    qt = q_all.transpose(3, 2, 4, 0, 1, 5).reshape(N_GROUPS, CW, 2 * SW)

    a_re = pw_re[CHUNK]
    a_im = pw_im[CHUNK]
    ar = jnp.concatenate([a_re, a_re], axis=-1)
    ai = jnp.concatenate([-a_im, a_im], axis=-1)
    return mt.astype(BF16), pt.astype(BF16), qt.astype(BF16), ar, ai


def _trunk(x, p, tm, nk):
    b, l, _ = x.shape
    nc = b * l // CHUNK
    kc = min(256, l // CHUNK)
    q_t, k, v_t = _qkv_call(x, p['g_mix0'], p['wqkv_t'], p['gq'], p['gk'], tm)
    o_t = _attn_call(q_t, k, v_t, p['hm'], p['bias'], p['rmask'])
    x_ph, h_ph = _attn_ffn_call(x, o_t, p['wo'], p['g_ffn0'], p['wgu0'], p['wd0'], p['g_mix1'], tm)

    sf, sb = _s5_state_call(h_ph, p['pt'], nk)
    sshape = (b, l // CHUNK, N_GROUPS, SW)
    xin_f = _scan_call(sf.reshape(sshape), p['ar'][0], p['ai'][0], kc, False)
    xin_b = _scan_call(sb.reshape(sshape), p['ar'][1], p['ai'][1], kc, True)
    z_pm = _s5_out_call(h_ph, xin_f.reshape(nc, N_GROUPS, SW), xin_b.reshape(nc, N_GROUPS, SW),
                        p['mt'], p['qt'], p['d_col'], nk)
    y = _glu_ffn_call(x_ph, z_pm, p['wglu'], p['g_ffn1'], p['wgu1'], p['wd1'], tm)
    return y.reshape(b, l, D_MODEL)


def kernel(x_prompt, x_sample, norm_mix, norm_ffn, w_qkv, w_o, q_gain, k_gain, rpb, lam_re, lam_im, log_step,
           b_re, b_im, c_re, c_im, d_skip, w_glu, w_gate_up, w_down):
    tm, nk = 512, 128
    scale = HEAD_DIM ** -0.5 * LOG2E
    mt, pt, qt, ar, ai = _s5_matrices(lam_re[0], lam_im[0], log_step[0], b_re[0], b_im[0], c_re[0], c_im[0])
    bias, rmask = _attn_tables(rpb[0])
    rows = np.arange(PAIR_W)[None, :, None]
    hm = np.broadcast_to((rows // HEAD_DIM) == np.arange(2)[:, None, None], (2, PAIR_W, QB))
    p = dict(
        g_mix0=norm_mix[0].astype(F32)[None], g_mix1=norm_mix[1].astype(F32)[None],
        g_ffn0=norm_ffn[0].astype(F32)[None], g_ffn1=norm_ffn[1].astype(F32)[None],
        wqkv_t=w_qkv[0].T.astype(BF16), wo=w_o[0].astype(BF16),
        gq=jnp.broadcast_to((q_gain[0].astype(F32) * scale)[:, None], (HEAD_DIM, tm)),
        gk=jnp.broadcast_to(k_gain[0].astype(F32)[:, None], (HEAD_DIM, tm)),
        hm=jnp.asarray(hm, BF16), bias=bias, rmask=rmask,
        wgu0=w_gate_up[0].astype(BF16), wd0=w_down[0].astype(BF16),
        wgu1=w_gate_up[1].astype(BF16), wd1=w_down[1].astype(BF16),
        wglu=w_glu[0].astype(BF16),
        mt=mt, pt=pt, qt=qt, ar=ar, ai=ai,
        d_col=jnp.broadcast_to(d_skip[0].astype(F32)[:, None], (D_MODEL, nk)),
    )
    return _trunk(x_prompt, p, tm, nk), _trunk(x_sample, p, tm, nk)
```

```python
import functools

import numpy as np
import jax
import jax.numpy as jnp
from jax import lax
from jax.experimental import pallas as pl
from jax.experimental.pallas import tpu as pltpu

D_MODEL = 1024
GRID_W = 64
N_HEADS = 16
HEAD_DIM = D_MODEL // N_HEADS
WIN_ROWS = 8
WIN_COLS = 16
SSM_GROUP = 16
N_GROUPS = D_MODEL // SSM_GROUP
STATE = 64
D_FF = 2816
EPS = 1e-6

F32 = jnp.float32
BF16 = jnp.bfloat16

V7X_LANES = 128
V7X_VMEM_BYTES = 64 * 1024 * 1024
VMEM_LIMIT = (V7X_VMEM_BYTES * 7) // 8

HEAD_PAIRS = N_HEADS // 2
PAIR_W = 2 * HEAD_DIM
Q_ROWS = 4
QB = Q_ROWS * GRID_W
KB = 3 * QB
NEG = -1e30
ONES_ROWS = 16
LOG2E = 1.4426950408889634
RM_ROWS = 16

CHUNK = 16
CW = CHUNK * SSM_GROUP
SW = 2 * STATE
GROUP_BLOCK = 16
GB_W = GROUP_BLOCK * SSM_GROUP
N_GB = N_GROUPS // GROUP_BLOCK
FF_CHUNK = D_FF // 2
ROW_SPLIT = 2


def _cparams(sem):
    return pltpu.CompilerParams(dimension_semantics=sem, vmem_limit_bytes=VMEM_LIMIT)


def _rms(x, g):
    return x * lax.rsqrt(jnp.mean(x * x, axis=-1, keepdims=True) + EPS) * g


def _dot(a, b):
    return jnp.dot(a, b, preferred_element_type=F32)


def _dot_nt(a, b):
    return lax.dot_general(a, b, (((1,), (1,)), ((), ())), preferred_element_type=F32)


def _dot_tn(a, b):
    return lax.dot_general(a, b, (((0,), (0,)), ((), ())), preferred_element_type=F32)


def _qkv_kernel(x_ref, g_ref, w_ref, gq_ref, gk_ref, q_ref, k_ref, v_ref):
    tm = x_ref.shape[1] // ROW_SPLIT

    def head_norm(t, gain):
        t3 = t.reshape(N_HEADS, HEAD_DIM, tm)
        y = t3 * lax.rsqrt(jnp.mean(t3 * t3, axis=1, keepdims=True) + EPS)
        return y * gain[None]

    for r in range(ROW_SPLIT):
        tok = slice(r * tm, (r + 1) * tm)
        h = _rms(x_ref[0, tok, :], g_ref[...]).astype(BF16)
        qkv = _dot_nt(w_ref[...], h)
        q = head_norm(qkv[0:D_MODEL], gq_ref[...])
        k = head_norm(qkv[D_MODEL:2 * D_MODEL], gk_ref[...])
        v = qkv[2 * D_MODEL:3 * D_MODEL]
        q_ref[0, :, :, tok] = q.reshape(HEAD_PAIRS, PAIR_W, tm).astype(BF16)
        v_ref[0, :, :, tok] = v.reshape(HEAD_PAIRS, PAIR_W, tm).astype(BF16)
        kt = k.reshape(D_MODEL, tm).T
        for j in range(HEAD_PAIRS):
            k_ref[0, j, tok, :] = kt[:, PAIR_W * j:PAIR_W * (j + 1)].astype(BF16)


def _qkv_call(x, g, w_t, gq, gk, tm):
    b, l, _ = x.shape
    return pl.pallas_call(
        _qkv_kernel,
        grid=(b, l // tm),
        in_specs=[
            pl.BlockSpec((1, tm, D_MODEL), lambda i, j: (i, j, 0)),
            pl.BlockSpec((1, D_MODEL), lambda i, j: (0, 0)),
            pl.BlockSpec((3 * D_MODEL, D_MODEL), lambda i, j: (0, 0)),
            pl.BlockSpec((HEAD_DIM, tm // ROW_SPLIT), lambda i, j: (0, 0)),
            pl.BlockSpec((HEAD_DIM, tm // ROW_SPLIT), lambda i, j: (0, 0)),
        ],
        out_specs=[
            pl.BlockSpec((1, HEAD_PAIRS, PAIR_W, tm), lambda i, j: (i, 0, 0, j)),
            pl.BlockSpec((1, HEAD_PAIRS, tm, PAIR_W), lambda i, j: (i, 0, j, 0)),
            pl.BlockSpec((1, HEAD_PAIRS, PAIR_W, tm), lambda i, j: (i, 0, 0, j)),
        ],
        out_shape=[
            jax.ShapeDtypeStruct((b, HEAD_PAIRS, PAIR_W, l), BF16),
            jax.ShapeDtypeStruct((b, HEAD_PAIRS, l, PAIR_W), BF16),
            jax.ShapeDtypeStruct((b, HEAD_PAIRS, PAIR_W, l), BF16),
        ],
        compiler_params=_cparams(("parallel", "parallel")),
        name="qkv_proj",
    )(x, g, w_t, gq, gk)


def _attn_kernel(q_ref, kp_ref, kc_ref, kn_ref, vp_ref, vc_ref, vn_ref, hm_ref, bias_ref, rm_ref, o_ref,
                 sa_sc, sb_sc, ma_sc, mb_sc):
    ones = jnp.ones((ONES_ROWS, QB), BF16)

    def scores(j, s_sc, m_sc):
        q2 = q_ref[0, j]
        for e in range(2):
            qm = q2 * hm_ref[e]
            mx = None
            for blk, k_ref in enumerate((kp_ref, kc_ref, kn_ref)):
                rows = slice(blk * QB, (blk + 1) * QB)
                s = _dot(k_ref[0, j], qm) + bias_ref[j, e, rows, :]
                s = s.reshape(Q_ROWS, GRID_W, QB) + rm_ref[0, Q_ROWS * blk:Q_ROWS * (blk + 1), :][:, None, :]
                s_sc[e, rows, :] = s.reshape(QB, QB)
                pm = jnp.max(s, axis=0)
                mx = pm if mx is None else jnp.maximum(mx, pm)
            m_sc[e] = mx

    def finish(j, s_sc, m_sc):
        for e in range(2):
            lo, hi = HEAD_DIM * e, HEAD_DIM * (e + 1)
            m = jnp.max(m_sc[e], axis=0, keepdims=True)
            pb = jnp.exp2(s_sc[e] - m).astype(BF16)
            o = None
            for blk, v_ref in enumerate((vp_ref, vc_ref, vn_ref)):
                v = jnp.concatenate([v_ref[0, j, lo:hi, :], ones], axis=0)
                ob = _dot(v, pb[blk * QB:(blk + 1) * QB])
                o = ob if o is None else o + ob
            o_ref[0, j, lo:hi, :] = (o[:HEAD_DIM] / o[HEAD_DIM:HEAD_DIM + 1]).astype(BF16)

    scores(0, sa_sc, ma_sc)

    def body(i, carry):
        j = 2 * i
        scores(j + 1, sb_sc, mb_sc)
        finish(j, sa_sc, ma_sc)
        scores(j + 2, sa_sc, ma_sc)
        finish(j + 1, sb_sc, mb_sc)
        return carry

    lax.fori_loop(0, HEAD_PAIRS // 2 - 1, body, 0)
    scores(HEAD_PAIRS - 1, sb_sc, mb_sc)
    finish(HEAD_PAIRS - 2, sa_sc, ma_sc)
    finish(HEAD_PAIRS - 1, sb_sc, mb_sc)


def _attn_call(q_t, k, v_t, hm, bias, rmask):
    b, _, _, l = q_t.shape
    nblk = l // QB
    prev = lambda j: jnp.maximum(j - 1, 0)
    nxt = lambda j: jnp.minimum(j + 1, nblk - 1)
    variant = lambda j: (j > 0).astype(jnp.int32) + (j == nblk - 1).astype(jnp.int32)
    kspec = lambda f: pl.BlockSpec((1, HEAD_PAIRS, QB, PAIR_W), lambda i, j: (i, 0, f(j), 0))
    vspec = lambda f: pl.BlockSpec((1, HEAD_PAIRS, PAIR_W, QB), lambda i, j: (i, 0, 0, f(j)))
    same = lambda j: j
    return pl.pallas_call(
        _attn_kernel,
        grid=(b, nblk),
        in_specs=[
            vspec(same),
            kspec(prev), kspec(same), kspec(nxt),
            vspec(prev), vspec(same), vspec(nxt),
            pl.BlockSpec((2, PAIR_W, QB), lambda i, j: (0, 0, 0)),
            pl.BlockSpec((HEAD_PAIRS, 2, KB, QB), lambda i, j: (0, 0, 0, 0), pipeline_mode=pl.Buffered(1)),
            pl.BlockSpec((1, RM_ROWS, QB), lambda i, j: (variant(j), 0, 0)),
        ],
        out_specs=vspec(same),
        out_shape=jax.ShapeDtypeStruct((b, HEAD_PAIRS, PAIR_W, l), BF16),
        scratch_shapes=[pltpu.VMEM((2, KB, QB), F32)] * 2 + [pltpu.VMEM((2, GRID_W, QB), F32)] * 2,
        compiler_params=_cparams(("parallel", "parallel")),
        name="nbr_attention",
    )(q_t, k, k, k, v_t, v_t, v_t, hm, bias, rmask)


def _attn_tables(rpb):
    kr = np.arange(3 * Q_ROWS)[:, None, None, None]
    kc = np.arange(GRID_W)[None, :, None, None]
    r = np.arange(Q_ROWS)[None, None, :, None]
    c = np.arange(GRID_W)[None, None, None, :]
    shape = (3 * Q_ROWS, GRID_W, Q_ROWS, GRID_W)
    cs = np.clip(c - WIN_COLS // 2, 0, GRID_W - WIN_COLS)
    col_ok = ((kc >= cs) & (kc < cs + WIN_COLS))[0, :, 0, :]
    dc = (kc - c + WIN_COLS - 1)[0, :, 0, :]
    onehot = ((dc[None] == np.arange(2 * WIN_COLS - 1)[:, None, None]) & col_ok[None]).astype(np.float32)
    toep = jnp.einsum('hrd,dkc->hrkc', rpb.astype(F32), onehot, precision=lax.Precision.HIGHEST)
    toep = jnp.where(col_ok[None, None], toep * LOG2E, NEG)
    off = WIN_ROWS - 1 - Q_ROWS
    bias = jnp.concatenate([toep[:, off - rr:off - rr + 3 * Q_ROWS] for rr in range(Q_ROWS)], axis=-1)
    bias = bias.reshape(HEAD_PAIRS, 2, KB, QB)
    rshape = (3 * Q_ROWS, 1, Q_ROWS, GRID_W)
    first = np.broadcast_to(kr >= Q_ROWS, rshape)
    inner = np.broadcast_to((kr >= r) & (kr < r + WIN_ROWS), rshape)
    last = np.broadcast_to(kr < 2 * Q_ROWS, rshape)
    rmask = np.where(np.stack([first, inner, last]), 0.0, NEG).astype(np.float32).reshape(3, 3 * Q_ROWS, QB)
    rmask = np.pad(rmask, ((0, 0), (0, RM_ROWS - 3 * Q_ROWS), (0, 0)))
    return bias, jnp.asarray(rmask)


def _ffn(x0, gf_ref, wgu_ref, wd_ref):
    hb = _rms(x0, gf_ref[...]).astype(BF16)
    y = x0
    for c in range(D_FF // FF_CHUNK):
        g = _dot(hb, wgu_ref[:, FF_CHUNK * c:FF_CHUNK * (c + 1)])
        u = _dot(hb, wgu_ref[:, D_FF + FF_CHUNK * c:D_FF + FF_CHUNK * (c + 1)])
        act = (g * (1.0 / (1.0 + jnp.exp(-g))) * u).astype(BF16)
        y = y + _dot(act, wd_ref[FF_CHUNK * c:FF_CHUNK * (c + 1), :])
    return y


def _resident(shape):
    return pl.BlockSpec(shape, lambda *idx: (0,) * len(shape), pipeline_mode=pl.Buffered(1))


def _phase_major(x):
    return jnp.swapaxes(x.reshape(-1, CHUNK, D_MODEL), 0, 1)


def _store_chunk_major(ref, x):
    xs = _phase_major(x)
    for s in range(CHUNK):
        ref[:, D_MODEL * s:D_MODEL * (s + 1)] = xs[s]


def _attn_ffn_kernel(x_ref, o_ref, wo_ref, gf_ref, wgu_ref, wd_ref, gn_ref, x1_ref, h1_ref):
    o = o_ref[0].reshape(D_MODEL, -1)
    tm = o.shape[1]
    parts = []
    for r in range(ROW_SPLIT):
        rows = slice(r * tm // ROW_SPLIT, (r + 1) * tm // ROW_SPLIT)
        x0 = x_ref[0, rows, :] + _dot_tn(o[:, rows], wo_ref[...])
        parts.append(_ffn(x0, gf_ref, wgu_ref, wd_ref))
    x1 = jnp.concatenate(parts, axis=0)
    _store_chunk_major(x1_ref, x1)
    _store_chunk_major(h1_ref, _rms(x1, gn_ref[...]))


def _attn_ffn_call(x, o_t, wo, gf, wgu, wd, gn, tm):
    b, l, _ = x.shape
    nt = l // tm
    out = pl.BlockSpec((tm // CHUNK, CHUNK * D_MODEL), lambda i, j: (i * nt + j, 0))
    return pl.pallas_call(
        _attn_ffn_kernel,
        grid=(b, nt),
        in_specs=[
            pl.BlockSpec((1, tm, D_MODEL), lambda i, j: (i, j, 0)),
            pl.BlockSpec((1, HEAD_PAIRS, PAIR_W, tm), lambda i, j: (i, 0, 0, j)),
            _resident((D_MODEL, D_MODEL)),
            _resident((1, D_MODEL)),
            _resident((D_MODEL, 2 * D_FF)),
            _resident((D_FF, D_MODEL)),
            _resident((1, D_MODEL)),
        ],
        out_specs=[out, out],
        out_shape=[jax.ShapeDtypeStruct((b * l // CHUNK, CHUNK * D_MODEL), F32)] * 2,
        compiler_params=_cparams(("parallel", "parallel")),
        name="attn_out_ffn",
    )(x, o_t, wo, gf, wgu, wd, gn)


def _glu_ffn_kernel(x_ref, z_ref, wglu_ref, gf_ref, wgu_ref, wd_ref, y_ref):
    parts = []
    for r in range(ROW_SPLIT):
        ph = range(r * CHUNK // ROW_SPLIT, (r + 1) * CHUNK // ROW_SPLIT)
        z = z_ref[ph.start:ph.stop].reshape(-1, D_MODEL)
        x = jnp.concatenate([x_ref[:, D_MODEL * s:D_MODEL * (s + 1)] for s in ph], axis=0)
        zz = _dot(z, wglu_ref[...])
        za, zg = zz[:, :D_MODEL], zz[:, D_MODEL:]
        x0 = x + za * (1.0 / (1.0 + jnp.exp(-zg)))
        parts.append(_ffn(x0, gf_ref, wgu_ref, wd_ref))
    y = jnp.concatenate(parts, axis=0).reshape(CHUNK, -1, D_MODEL)
    y_ref[...] = jnp.swapaxes(y, 0, 1).reshape(-1, D_MODEL)


def _glu_ffn_call(x_ph, z_pm, wglu, gf, wgu, wd, tm):
    nc = x_ph.shape[0]
    nk = tm // CHUNK
    return pl.pallas_call(
        _glu_ffn_kernel,
        grid=(nc // nk,),
        in_specs=[
            pl.BlockSpec((nk, CHUNK * D_MODEL), lambda i: (i, 0)),
            pl.BlockSpec((CHUNK, nk, D_MODEL), lambda i: (0, i, 0)),
            _resident((D_MODEL, 2 * D_MODEL)),
            _resident((1, D_MODEL)),
            _resident((D_MODEL, 2 * D_FF)),
            _resident((D_FF, D_MODEL)),
        ],
        out_specs=pl.BlockSpec((tm, D_MODEL), lambda i: (i, 0)),
        out_shape=jax.ShapeDtypeStruct((nc * CHUNK, D_MODEL), F32),
        compiler_params=_cparams(("parallel",)),
        name="glu_out_ffn",
    )(x_ph, z_pm, wglu, gf, wgu, wd)


def _chunk_operands(h_refs, ut_sc, ht_sc):
    for s, h_ref in enumerate(h_refs):
        ht = h_ref[...].T
        if ht_sc is not None:
            ht_sc[s] = ht
        ut_sc[:, SSM_GROUP * s:SSM_GROUP * (s + 1), :] = (
            ht.reshape(GROUP_BLOCK, SSM_GROUP, -1).astype(BF16))


def _s5_state_kernel(*refs):
    h_refs = refs[:CHUNK]
    pt_ref, sf_ref, sb_ref, ut_sc, st_sc = refs[CHUNK:]
    _chunk_operands(h_refs, ut_sc, None)
    for g in range(GROUP_BLOCK):
        st_sc[g] = _dot(pt_ref[g], ut_sc[g]).T
    sf_ref[...] = jnp.swapaxes(st_sc[:, :, :SW], 0, 1)
    sb_ref[...] = jnp.swapaxes(st_sc[:, :, SW:], 0, 1)


def _h_specs(nk):
    return [pl.BlockSpec((nk, GB_W), lambda gb, i, s=s: (i, s * N_GB + gb)) for s in range(CHUNK)]


def _s5_state_call(h_ph, pt, nk):
    nc = h_ph.shape[0]
    sspec = pl.BlockSpec((nk, GROUP_BLOCK, SW), lambda gb, i: (i, gb, 0))
    return pl.pallas_call(
        _s5_state_kernel,
        grid=(N_GB, nc // nk),
        in_specs=[*_h_specs(nk),
                  pl.BlockSpec((GROUP_BLOCK, 2 * SW, CW), lambda gb, i: (gb, 0, 0))],
        out_specs=[sspec, sspec],
        out_shape=[jax.ShapeDtypeStruct((nc, N_GROUPS, SW), F32)] * 2,
        scratch_shapes=[pltpu.VMEM((GROUP_BLOCK, CW, nk), BF16),
                        pltpu.VMEM((GROUP_BLOCK, nk, 2 * SW), F32)],
        compiler_params=_cparams(("parallel", "parallel")),
        name="s5_chunk_states",
    )(*([h_ph] * CHUNK), pt)


def _scan_kernel(s_ref, ar_ref, ai_ref, xin_ref, x_sc, xs_sc, *, reverse):
    kc = s_ref.shape[1]

    @pl.when(pl.program_id(1) == 0)
    def _():
        x_sc[...] = jnp.zeros_like(x_sc)
        xs_sc[...] = jnp.zeros_like(xs_sc)

    ar = ar_ref[...]
    ai = ai_ref[...]

    def body(i, carry):
        x, xs = carry
        k = kc - 1 - i if reverse else i
        s = s_ref[0, k]
        xin_ref[0, k] = x.astype(BF16)
        ss = pltpu.roll(s, STATE, 1)
        return ar * x + ai * xs + s, ar * xs - ai * x + ss

    x, xs = lax.fori_loop(0, kc, body, (x_sc[...], xs_sc[...]), unroll=8)
    x_sc[...] = x
    xs_sc[...] = xs


def _scan_call(s, ar, ai, kc, reverse):
    b, ncs = s.shape[:2]
    nkb = ncs // kc
    blk = (lambda i, j: (i, nkb - 1 - j, 0, 0)) if reverse else (lambda i, j: (i, j, 0, 0))
    tab = pl.BlockSpec((N_GROUPS, SW), lambda i, j: (0, 0))
    return pl.pallas_call(
        functools.partial(_scan_kernel, reverse=reverse),
        grid=(b, nkb),
        in_specs=[pl.BlockSpec((1, kc, N_GROUPS, SW), blk), tab, tab],
        out_specs=pl.BlockSpec((1, kc, N_GROUPS, SW), blk),
        out_shape=jax.ShapeDtypeStruct(s.shape, BF16),
        scratch_shapes=[pltpu.VMEM((N_GROUPS, SW), F32)] * 2,
        compiler_params=_cparams(("parallel", "arbitrary")),
        name="s5_chunk_scan_bwd" if reverse else "s5_chunk_scan_fwd",
    )(s, ar, ai)


def _s5_out_kernel(*refs):
    h_refs = refs[:CHUNK]
    xf_ref, xb_ref, mt_ref, qt_ref, d_ref, z_ref, ut_sc, ht_sc, yt_sc = refs[CHUNK:]
    _chunk_operands(h_refs, ut_sc, ht_sc)
    xf = jnp.swapaxes(xf_ref[...].astype(F32), 0, 1)
    xb = jnp.swapaxes(xb_ref[...].astype(F32), 0, 1)
    for g in range(GROUP_BLOCK):
        xin = jnp.concatenate([xf[g], xb[g]], axis=1).astype(BF16)
        yt_sc[g] = _dot(mt_ref[g], ut_sc[g]) + _dot_nt(qt_ref[g], xin)
    d = d_ref[...]
    for t in range(CHUNK):
        y = yt_sc[:, SSM_GROUP * t:SSM_GROUP * (t + 1), :].reshape(GB_W, -1)
        v = y + d * ht_sc[t]
        z_ref[t] = jax.nn.gelu(v, approximate=True).T.astype(BF16)


def _s5_out_call(h_ph, xin_f, xin_b, mt, qt, d_col, nk):
    nc = h_ph.shape[0]
    xspec = pl.BlockSpec((nk, GROUP_BLOCK, SW), lambda gb, i: (i, gb, 0))
    wspec = pl.BlockSpec((GROUP_BLOCK, CW, CW), lambda gb, i: (gb, 0, 0))
    return pl.pallas_call(
        _s5_out_kernel,
        grid=(N_GB, nc // nk),
        in_specs=[*_h_specs(nk), xspec, xspec, wspec, wspec,
                  pl.BlockSpec((GB_W, nk), lambda gb, i: (gb, 0))],
        out_specs=pl.BlockSpec((CHUNK, nk, GB_W), lambda gb, i: (0, i, gb)),
        out_shape=jax.ShapeDtypeStruct((CHUNK, nc, D_MODEL), BF16),
        scratch_shapes=[
            pltpu.VMEM((GROUP_BLOCK, CW, nk), BF16),
            pltpu.VMEM((CHUNK, GB_W, nk), F32),
            pltpu.VMEM((GROUP_BLOCK, CW, nk), F32),
        ],
        compiler_params=_cparams(("parallel", "parallel")),
        name="s5_outputs",
    )(*([h_ph] * CHUNK), xin_f, xin_b, mt, qt, d_col)


def _s5_matrices(lam_re, lam_im, log_step, b_re, b_im, c_re, c_im):
    hp = lax.Precision.HIGHEST
    dt = jnp.exp(log_step.astype(F32))[..., None]
    lam_re = lam_re.astype(F32)
    lam_im = lam_im.astype(F32)
    mag = jnp.exp(lam_re * dt)
    ang = lam_im * dt
    lb_re = mag * jnp.cos(ang)
    lb_im = mag * jnp.sin(ang)
    den = lam_re * lam_re + lam_im * lam_im
    nr = lb_re - 1.0
    ni = lb_im
    coef_re = (nr * lam_re + ni * lam_im) / den
    coef_im = (ni * lam_re - nr * lam_im) / den
    b_re = b_re.astype(F32)
    b_im = b_im.astype(F32)
    bb_re = coef_re[..., None] * b_re - coef_im[..., None] * b_im
    bb_im = coef_re[..., None] * b_im + coef_im[..., None] * b_re
    c_re = c_re.astype(F32)
    c_im = c_im.astype(F32)

    pw_re = [jnp.ones_like(lb_re)]
    pw_im = [jnp.zeros_like(lb_im)]
    for _ in range(CHUNK):
        pr, pi = pw_re[-1], pw_im[-1]
        pw_re.append(pr * lb_re - pi * lb_im)
        pw_im.append(pr * lb_im + pi * lb_re)
    pw_re = jnp.stack(pw_re, axis=-1)
    pw_im = jnp.stack(pw_im, axis=-1)

    bbl_re = jnp.tile(bb_re, (1, 1, 1, CHUNK))
    bbl_im = jnp.tile(bb_im, (1, 1, 1, CHUNK))

    def response(pr, pi):
        prl = jnp.repeat(pr, SSM_GROUP, axis=-1)
        pil = jnp.repeat(pi, SSM_GROUP, axis=-1)
        return prl * bbl_re - pil * bbl_im, prl * bbl_im + pil * bbl_re

    w_re, w_im = response(pw_re[..., :CHUNK], pw_im[..., :CHUNK])
    wr_re, wr_im = response(pw_re[..., CHUNK - 1::-1], pw_im[..., CHUNK - 1::-1])

    kmat = (jnp.einsum('dgcn,dgnx->dgcx', c_re, w_re, precision=hp)
            - jnp.einsum('dgcn,dgnx->dgcx', c_im, w_im, precision=hp))
    lane = jnp.arange(CW)
    step, chan = lane // SSM_GROUP, lane % SSM_GROUP
    same_c = chan[:, None] == chan[None, :]
    t_i = jnp.arange(CHUNK)[:, None, None]
    sel_f = (same_c[None] & (step[None, :, None] == t_i - step[None, None, :])).astype(F32)
    sel_b = (same_c[None] & (step[None, :, None] == step[None, None, :] - t_i)).astype(F32)
    mt = (jnp.einsum('gcx,txy->gtcy', kmat[0], sel_f, precision=hp)
          + jnp.einsum('gcx,txy->gtcy', kmat[1], sel_b, precision=hp)).reshape(N_GROUPS, CW, CW)

    pt = jnp.stack([wr_re[0], wr_im[0], w_re[1], w_im[1]], axis=1).reshape(N_GROUPS, 2 * SW, CW)

    def carry(d, pr, pi):
        cr = c_re[d][:, None]
        ci = c_im[d][:, None]
        pr = pr.transpose(0, 2, 1)[:, :, None, :]
        pi = pi.transpose(0, 2, 1)[:, :, None, :]
        return [cr * pr - ci * pi, -(cr * pi + ci * pr)]

    qt = jnp.concatenate(carry(0, pw_re[0, ..., 1:], pw_im[0, ..., 1:])
                         + carry(1, pw_re[1, ..., :0:-1], pw_im[1, ..., :0:-1]), axis=-1)
    qt = qt.reshape(N_GROUPS, CW, 2 * SW)

    a_re = pw_re[..., CHUNK]
    a_im = pw_im[..., CHUNK]
    ar = jnp.concatenate([a_re, a_re], axis=-1)
    ai = jnp.concatenate([-a_im, a_im], axis=-1)
    return mt.astype(BF16), pt.astype(BF16), qt.astype(BF16), ar, ai


def _trunk(x, p, tm, nk):
    b, l, _ = x.shape
    nc = b * l // CHUNK
    kc = min(256, l // CHUNK)
    q_t, k, v_t = _qkv_call(x, p['g_mix0'], p['wqkv_t'], p['gq'], p['gk'], tm)
    o_t = _attn_call(q_t, k, v_t, p['hm'], p['bias'], p['rmask'])
    x_ph, h_ph = _attn_ffn_call(x, o_t, p['wo'], p['g_ffn0'], p['wgu0'], p['wd0'], p['g_mix1'], tm)

    sf, sb = _s5_state_call(h_ph, p['pt'], nk)
    sshape = (b, l // CHUNK, N_GROUPS, SW)
    xin_f = _scan_call(sf.reshape(sshape), p['ar'][0], p['ai'][0], kc, False)
    xin_b = _scan_call(sb.reshape(sshape), p['ar'][1], p['ai'][1], kc, True)
    z_pm = _s5_out_call(h_ph, xin_f.reshape(nc, N_GROUPS, SW), xin_b.reshape(nc, N_GROUPS, SW),
                        p['mt'], p['qt'], p['d_col'], nk)
    y = _glu_ffn_call(x_ph, z_pm, p['wglu'], p['g_ffn1'], p['wgu1'], p['wd1'], tm)
    return y.reshape(b, l, D_MODEL)


def kernel(x_prompt, x_sample, norm_mix, norm_ffn, w_qkv, w_o, q_gain, k_gain, rpb, lam_re, lam_im, log_step,
           b_re, b_im, c_re, c_im, d_skip, w_glu, w_gate_up, w_down):
    tm, nk = 512, 128
    scale = HEAD_DIM ** -0.5 * LOG2E
    mt, pt, qt, ar, ai = _s5_matrices(lam_re[0], lam_im[0], log_step[0], b_re[0], b_im[0], c_re[0], c_im[0])
    bias, rmask = _attn_tables(rpb[0])
    rows = np.arange(PAIR_W)[None, :, None]
    hm = np.broadcast_to((rows // HEAD_DIM) == np.arange(2)[:, None, None], (2, PAIR_W, QB))
    p = dict(
        g_mix0=norm_mix[0].astype(F32)[None], g_mix1=norm_mix[1].astype(F32)[None],
        g_ffn0=norm_ffn[0].astype(F32)[None], g_ffn1=norm_ffn[1].astype(F32)[None],
        wqkv_t=w_qkv[0].T.astype(BF16), wo=w_o[0].astype(BF16),
        gq=jnp.broadcast_to((q_gain[0].astype(F32) * scale)[:, None], (HEAD_DIM, tm // ROW_SPLIT)),
        gk=jnp.broadcast_to(k_gain[0].astype(F32)[:, None], (HEAD_DIM, tm // ROW_SPLIT)),
        hm=jnp.asarray(hm, BF16), bias=bias, rmask=rmask,
        wgu0=w_gate_up[0].astype(BF16), wd0=w_down[0].astype(BF16),
        wgu1=w_gate_up[1].astype(BF16), wd1=w_down[1].astype(BF16),
        wglu=w_glu[0].astype(BF16),
        mt=mt, pt=pt, qt=qt, ar=ar, ai=ai,
        d_col=jnp.broadcast_to(d_skip[0].astype(F32)[:, None], (D_MODEL, nk)),
    )
    return _trunk(x_prompt, p, tm, nk), _trunk(x_sample, p, tm, nk)
```

```python
import functools

import numpy as np
import jax
import jax.numpy as jnp
from jax import lax
from jax.experimental import pallas as pl
from jax.experimental.pallas import tpu as pltpu

D_MODEL = 1024
GRID_W = 64
N_HEADS = 16
HEAD_DIM = D_MODEL // N_HEADS
WIN_ROWS = 8
WIN_COLS = 16
SSM_GROUP = 16
N_GROUPS = D_MODEL // SSM_GROUP
STATE = 64
D_FF = 2816
EPS = 1e-6

F32 = jnp.float32
BF16 = jnp.bfloat16

V7X_LANES = 128
V7X_VMEM_BYTES = 64 * 1024 * 1024
VMEM_LIMIT = (V7X_VMEM_BYTES * 7) // 8

HEAD_PAIRS = N_HEADS // 2
PAIR_W = 2 * HEAD_DIM
Q_ROWS = 4
QB = Q_ROWS * GRID_W
KB = 3 * QB
NEG = -1e30
ONES_ROWS = 16
LOG2E = 1.4426950408889634
RM_ROWS = 16

CHUNK = 16
CW = CHUNK * SSM_GROUP
SW = 2 * STATE
GROUP_BLOCK = 16
GB_W = GROUP_BLOCK * SSM_GROUP
N_GB = N_GROUPS // GROUP_BLOCK
ROW_SPLIT = 2


def _cparams(sem):
    return pltpu.CompilerParams(dimension_semantics=sem, vmem_limit_bytes=VMEM_LIMIT)


def _rms(x, g):
    return x * lax.rsqrt(jnp.mean(x * x, axis=-1, keepdims=True) + EPS) * g


def _dot(a, b):
    return jnp.dot(a, b, preferred_element_type=F32)


def _dot_nt(a, b):
    return lax.dot_general(a, b, (((1,), (1,)), ((), ())), preferred_element_type=F32)


def _dot_tn(a, b):
    return lax.dot_general(a, b, (((0,), (0,)), ((), ())), preferred_element_type=F32)


def _qkv_kernel(x_ref, g_ref, w_ref, gq_ref, gk_ref, q_ref, k_ref, v_ref):
    tm = x_ref.shape[1] // ROW_SPLIT

    def head_norm(t, gain):
        t3 = t.reshape(N_HEADS, HEAD_DIM, tm)
        y = t3 * lax.rsqrt(jnp.mean(t3 * t3, axis=1, keepdims=True) + EPS)
        return y * gain[None]

    for r in range(ROW_SPLIT):
        tok = slice(r * tm, (r + 1) * tm)
        h = _rms(x_ref[0, tok, :], g_ref[...]).astype(BF16)
        qkv = _dot_nt(w_ref[...], h)
        q = head_norm(qkv[0:D_MODEL], gq_ref[...])
        k = head_norm(qkv[D_MODEL:2 * D_MODEL], gk_ref[...])
        v = qkv[2 * D_MODEL:3 * D_MODEL]
        q_ref[0, r] = q.reshape(HEAD_PAIRS, PAIR_W, tm).astype(BF16)
        v_ref[0, r] = v.reshape(HEAD_PAIRS, PAIR_W, tm).astype(BF16)
        kt = k.reshape(D_MODEL, tm).T
        for j in range(HEAD_PAIRS):
            k_ref[0, j, tok, :] = kt[:, PAIR_W * j:PAIR_W * (j + 1)].astype(BF16)


def _qkv_call(x, g, w_t, gq, gk, tm):
    b, l, _ = x.shape
    assert tm // ROW_SPLIT == QB
    blocked = pl.BlockSpec((1, ROW_SPLIT, HEAD_PAIRS, PAIR_W, QB), lambda i, j: (i, j, 0, 0, 0))
    return pl.pallas_call(
        _qkv_kernel,
        grid=(b, l // tm),
        in_specs=[
            pl.BlockSpec((1, tm, D_MODEL), lambda i, j: (i, j, 0)),
            pl.BlockSpec((1, D_MODEL), lambda i, j: (0, 0)),
            pl.BlockSpec((3 * D_MODEL, D_MODEL), lambda i, j: (0, 0)),
            pl.BlockSpec((HEAD_DIM, tm // ROW_SPLIT), lambda i, j: (0, 0)),
            pl.BlockSpec((HEAD_DIM, tm // ROW_SPLIT), lambda i, j: (0, 0)),
        ],
        out_specs=[
            blocked,
            pl.BlockSpec((1, HEAD_PAIRS, tm, PAIR_W), lambda i, j: (i, 0, j, 0)),
            blocked,
        ],
        out_shape=[
            jax.ShapeDtypeStruct((b, l // QB, HEAD_PAIRS, PAIR_W, QB), BF16),
            jax.ShapeDtypeStruct((b, HEAD_PAIRS, l, PAIR_W), BF16),
            jax.ShapeDtypeStruct((b, l // QB, HEAD_PAIRS, PAIR_W, QB), BF16),
        ],
        compiler_params=_cparams(("parallel", "parallel")),
        name="qkv_proj",
    )(x, g, w_t, gq, gk)


def _attn_kernel(q_ref, kp_ref, kc_ref, kn_ref, vp_ref, vc_ref, vn_ref, hm_ref, bias_ref, rm_ref, o_ref,
                 sa_sc, sb_sc, ma_sc, mb_sc):
    ones = jnp.ones((ONES_ROWS, QB), BF16)

    def scores(j, s_sc, m_sc):
        q2 = q_ref[0, 0, j]
        for e in range(2):
            qm = q2 * hm_ref[e]
            mx = None
            for blk, k_ref in enumerate((kp_ref, kc_ref, kn_ref)):
                rows = slice(blk * QB, (blk + 1) * QB)
                s = _dot(k_ref[0, j], qm) + bias_ref[j, e, rows, :]
                s = s.reshape(Q_ROWS, GRID_W, QB) + rm_ref[0, Q_ROWS * blk:Q_ROWS * (blk + 1), :][:, None, :]
                s_sc[e, rows, :] = s.reshape(QB, QB)
                pm = jnp.max(s, axis=0)
                mx = pm if mx is None else jnp.maximum(mx, pm)
            m_sc[e] = mx

    def finish(j, s_sc, m_sc):
        for e in range(2):
            lo, hi = HEAD_DIM * e, HEAD_DIM * (e + 1)
            m = jnp.max(m_sc[e], axis=0, keepdims=True)
            pb = jnp.exp2(s_sc[e] - m).astype(BF16)
            o = None
            for blk, v_ref in enumerate((vp_ref, vc_ref, vn_ref)):
                v = jnp.concatenate([v_ref[0, 0, j, lo:hi, :], ones], axis=0)
                ob = _dot(v, pb[blk * QB:(blk + 1) * QB])
                o = ob if o is None else o + ob
            o_ref[0, 0, j, lo:hi, :] = (o[:HEAD_DIM] / o[HEAD_DIM:HEAD_DIM + 1]).astype(BF16)

    scores(0, sa_sc, ma_sc)

    def body(i, carry):
        j = 2 * i
        scores(j + 1, sb_sc, mb_sc)
        finish(j, sa_sc, ma_sc)
        scores(j + 2, sa_sc, ma_sc)
        finish(j + 1, sb_sc, mb_sc)
        return carry

    lax.fori_loop(0, HEAD_PAIRS // 2 - 1, body, 0)
    scores(HEAD_PAIRS - 1, sb_sc, mb_sc)
    finish(HEAD_PAIRS - 2, sa_sc, ma_sc)
    finish(HEAD_PAIRS - 1, sb_sc, mb_sc)


def _attn_call(q_t, k, v_t, hm, bias, rmask):
    b, nblk = q_t.shape[:2]
    prev = lambda j: jnp.maximum(j - 1, 0)
    nxt = lambda j: jnp.minimum(j + 1, nblk - 1)
    variant = lambda j: (j > 0).astype(jnp.int32) + (j == nblk - 1).astype(jnp.int32)
    kspec = lambda f: pl.BlockSpec((1, HEAD_PAIRS, QB, PAIR_W), lambda i, j: (i, 0, f(j), 0))
    vspec = lambda f: pl.BlockSpec((1, 1, HEAD_PAIRS, PAIR_W, QB), lambda i, j: (i, f(j), 0, 0, 0))
    same = lambda j: j
    return pl.pallas_call(
        _attn_kernel,
        grid=(b, nblk),
        in_specs=[
            vspec(same),
            kspec(prev), kspec(same), kspec(nxt),
            vspec(prev), vspec(same), vspec(nxt),
            pl.BlockSpec((2, PAIR_W, QB), lambda i, j: (0, 0, 0)),
            pl.BlockSpec((HEAD_PAIRS, 2, KB, QB), lambda i, j: (0, 0, 0, 0), pipeline_mode=pl.Buffered(1)),
            pl.BlockSpec((1, RM_ROWS, QB), lambda i, j: (variant(j), 0, 0)),
        ],
        out_specs=vspec(same),
        out_shape=jax.ShapeDtypeStruct(q_t.shape, BF16),
        scratch_shapes=[pltpu.VMEM((2, KB, QB), F32)] * 2 + [pltpu.VMEM((2, GRID_W, QB), F32)] * 2,
        compiler_params=_cparams(("parallel", "parallel")),
        name="nbr_attention",
    )(q_t, k, k, k, v_t, v_t, v_t, hm, bias, rmask)


def _attn_tables(rpb):
    kr = np.arange(3 * Q_ROWS)[:, None, None, None]
    kc = np.arange(GRID_W)[None, :, None, None]
    r = np.arange(Q_ROWS)[None, None, :, None]
    c = np.arange(GRID_W)[None, None, None, :]
    shape = (3 * Q_ROWS, GRID_W, Q_ROWS, GRID_W)
    cs = np.clip(c - WIN_COLS // 2, 0, GRID_W - WIN_COLS)
    col_ok = ((kc >= cs) & (kc < cs + WIN_COLS))[0, :, 0, :]
    dc = (kc - c + WIN_COLS - 1)[0, :, 0, :]
    onehot = ((dc[None] == np.arange(2 * WIN_COLS - 1)[:, None, None]) & col_ok[None]).astype(np.float32)
    toep = jnp.einsum('hrd,dkc->hrkc', rpb.astype(F32), onehot, precision=lax.Precision.HIGHEST)
    toep = jnp.where(col_ok[None, None], toep * LOG2E, NEG)
    off = WIN_ROWS - 1 - Q_ROWS
    bias = jnp.concatenate([toep[:, off - rr:off - rr + 3 * Q_ROWS] for rr in range(Q_ROWS)], axis=-1)
    bias = bias.reshape(HEAD_PAIRS, 2, KB, QB)
    rshape = (3 * Q_ROWS, 1, Q_ROWS, GRID_W)
    first = np.broadcast_to(kr >= Q_ROWS, rshape)
    inner = np.broadcast_to((kr >= r) & (kr < r + WIN_ROWS), rshape)
    last = np.broadcast_to(kr < 2 * Q_ROWS, rshape)
    rmask = np.where(np.stack([first, inner, last]), 0.0, NEG).astype(np.float32).reshape(3, 3 * Q_ROWS, QB)
    rmask = np.pad(rmask, ((0, 0), (0, RM_ROWS - 3 * Q_ROWS), (0, 0)))
    return bias, jnp.asarray(rmask)


def _ffn(x0, gf_ref, wgu_ref, wd_ref):
    hb = _rms(x0, gf_ref[...]).astype(BF16)
    gu = _dot(hb, wgu_ref[...])
    g, u = gu[:, :D_FF], gu[:, D_FF:]
    act = (g * (1.0 / (1.0 + jnp.exp(-g))) * u).astype(BF16)
    return x0 + _dot(act, wd_ref[...])


def _resident(shape):
    return pl.BlockSpec(shape, lambda *idx: (0,) * len(shape), pipeline_mode=pl.Buffered(1))


def _phase_major(x):
    return jnp.swapaxes(x.reshape(-1, CHUNK, D_MODEL), 0, 1)


def _store_chunk_major(ref, x):
    xs = _phase_major(x)
    for s in range(CHUNK):
        ref[:, D_MODEL * s:D_MODEL * (s + 1)] = xs[s]


def _attn_ffn_kernel(x_ref, o_ref, wo_ref, gf_ref, wgu_ref, wd_ref, gn_ref, x1_ref, h1_ref):
    parts = []
    for r in range(ROW_SPLIT):
        o = o_ref[0, r].reshape(D_MODEL, QB)
        x0 = x_ref[0, QB * r:QB * (r + 1), :] + _dot_tn(o, wo_ref[...])
        parts.append(_ffn(x0, gf_ref, wgu_ref, wd_ref))
    x1 = jnp.concatenate(parts, axis=0)
    _store_chunk_major(x1_ref, x1)
    _store_chunk_major(h1_ref, _rms(x1, gn_ref[...]))


def _attn_ffn_call(x, o_t, wo, gf, wgu, wd, gn, tm):
    b, l, _ = x.shape
    nt = l // tm
    out = pl.BlockSpec((tm // CHUNK, CHUNK * D_MODEL), lambda i, j: (i * nt + j, 0))
    return pl.pallas_call(
        _attn_ffn_kernel,
        grid=(b, nt),
        in_specs=[
            pl.BlockSpec((1, tm, D_MODEL), lambda i, j: (i, j, 0)),
            pl.BlockSpec((1, ROW_SPLIT, HEAD_PAIRS, PAIR_W, QB), lambda i, j: (i, j, 0, 0, 0)),
            _resident((D_MODEL, D_MODEL)),
            _resident((1, D_MODEL)),
            _resident((D_MODEL, 2 * D_FF)),
            _resident((D_FF, D_MODEL)),
            _resident((1, D_MODEL)),
        ],
        out_specs=[out, out],
        out_shape=[jax.ShapeDtypeStruct((b * l // CHUNK, CHUNK * D_MODEL), F32)] * 2,
        compiler_params=_cparams(("parallel", "parallel")),
        name="attn_out_ffn",
    )(x, o_t, wo, gf, wgu, wd, gn)


def _glu_ffn_kernel(x_ref, z_ref, wglu_ref, gf_ref, wgu_ref, wd_ref, y_ref):
    parts = []
    for r in range(ROW_SPLIT):
        ph = range(r * CHUNK // ROW_SPLIT, (r + 1) * CHUNK // ROW_SPLIT)
        z = z_ref[ph.start:ph.stop].reshape(-1, D_MODEL)
        x = jnp.concatenate([x_ref[:, D_MODEL * s:D_MODEL * (s + 1)] for s in ph], axis=0)
        zz = _dot(z, wglu_ref[...])
        za, zg = zz[:, :D_MODEL], zz[:, D_MODEL:]
        x0 = x + za * (1.0 / (1.0 + jnp.exp(-zg)))
        parts.append(_ffn(x0, gf_ref, wgu_ref, wd_ref))
    y = jnp.concatenate(parts, axis=0).reshape(CHUNK, -1, D_MODEL)
    y_ref[...] = jnp.swapaxes(y, 0, 1).reshape(-1, D_MODEL)


def _glu_ffn_call(x_ph, z_pm, wglu, gf, wgu, wd, tm):
    nc = x_ph.shape[0]
    nk = tm // CHUNK
    return pl.pallas_call(
        _glu_ffn_kernel,
        grid=(nc // nk,),
        in_specs=[
            pl.BlockSpec((nk, CHUNK * D_MODEL), lambda i: (i, 0)),
            pl.BlockSpec((CHUNK, nk, D_MODEL), lambda i: (0, i, 0)),
            _resident((D_MODEL, 2 * D_MODEL)),
            _resident((1, D_MODEL)),
            _resident((D_MODEL, 2 * D_FF)),
            _resident((D_FF, D_MODEL)),
        ],
        out_specs=pl.BlockSpec((tm, D_MODEL), lambda i: (i, 0)),
        out_shape=jax.ShapeDtypeStruct((nc * CHUNK, D_MODEL), F32),
        compiler_params=_cparams(("parallel",)),
        name="glu_out_ffn",
    )(x_ph, z_pm, wglu, gf, wgu, wd)


def _chunk_operands(h_refs, ut_sc, ht_sc):
    for s, h_ref in enumerate(h_refs):
        ht = h_ref[...].T
        if ht_sc is not None:
            ht_sc[s] = ht
        ut_sc[:, SSM_GROUP * s:SSM_GROUP * (s + 1), :] = (
            ht.reshape(GROUP_BLOCK, SSM_GROUP, -1).astype(BF16))


def _s5_state_kernel(*refs):
    h_refs = refs[:CHUNK]
    pt_ref, sf_ref, sb_ref, ut_sc, st_sc = refs[CHUNK:]
    _chunk_operands(h_refs, ut_sc, None)
    for g in range(GROUP_BLOCK):
        st_sc[g] = _dot(pt_ref[g], ut_sc[g]).T
    sf_ref[...] = jnp.swapaxes(st_sc[:, :, :SW], 0, 1)
    sb_ref[...] = jnp.swapaxes(st_sc[:, :, SW:], 0, 1)


def _h_specs(nk):
    return [pl.BlockSpec((nk, GB_W), lambda gb, i, s=s: (i, s * N_GB + gb)) for s in range(CHUNK)]


def _s5_state_call(h_ph, pt, nk):
    nc = h_ph.shape[0]
    sspec = pl.BlockSpec((nk, GROUP_BLOCK, SW), lambda gb, i: (i, gb, 0))
    return pl.pallas_call(
        _s5_state_kernel,
        grid=(N_GB, nc // nk),
        in_specs=[*_h_specs(nk),
                  pl.BlockSpec((GROUP_BLOCK, 2 * SW, CW), lambda gb, i: (gb, 0, 0))],
        out_specs=[sspec, sspec],
        out_shape=[jax.ShapeDtypeStruct((nc, N_GROUPS, SW), F32)] * 2,
        scratch_shapes=[pltpu.VMEM((GROUP_BLOCK, CW, nk), BF16),
                        pltpu.VMEM((GROUP_BLOCK, nk, 2 * SW), F32)],
        compiler_params=_cparams(("parallel", "parallel")),
        name="s5_chunk_states",
    )(*([h_ph] * CHUNK), pt)


def _scan_kernel(s_ref, ar_ref, ai_ref, xin_ref, x_sc, xs_sc, *, reverse):
    kc = s_ref.shape[1]

    @pl.when(pl.program_id(1) == 0)
    def _():
        x_sc[...] = jnp.zeros_like(x_sc)
        xs_sc[...] = jnp.zeros_like(xs_sc)

    ar = ar_ref[...]
    ai = ai_ref[...]

    def body(i, carry):
        x, xs = carry
        k = kc - 1 - i if reverse else i
        s = s_ref[0, k]
        xin_ref[0, k] = x.astype(BF16)
        ss = pltpu.roll(s, STATE, 1)
        return ar * x + ai * xs + s, ar * xs - ai * x + ss

    x, xs = lax.fori_loop(0, kc, body, (x_sc[...], xs_sc[...]), unroll=8)
    x_sc[...] = x
    xs_sc[...] = xs


def _scan_call(s, ar, ai, kc, reverse):
    b, ncs = s.shape[:2]
    nkb = ncs // kc
    blk = (lambda i, j: (i, nkb - 1 - j, 0, 0)) if reverse else (lambda i, j: (i, j, 0, 0))
    tab = pl.BlockSpec((N_GROUPS, SW), lambda i, j: (0, 0))
    return pl.pallas_call(
        functools.partial(_scan_kernel, reverse=reverse),
        grid=(b, nkb),
        in_specs=[pl.BlockSpec((1, kc, N_GROUPS, SW), blk), tab, tab],
        out_specs=pl.BlockSpec((1, kc, N_GROUPS, SW), blk),
        out_shape=jax.ShapeDtypeStruct(s.shape, BF16),
        scratch_shapes=[pltpu.VMEM((N_GROUPS, SW), F32)] * 2,
        compiler_params=_cparams(("parallel", "arbitrary")),
        name="s5_chunk_scan_bwd" if reverse else "s5_chunk_scan_fwd",
    )(s, ar, ai)


def _s5_out_kernel(*refs):
    h_refs = refs[:CHUNK]
    xf_ref, xb_ref, mt_ref, qt_ref, d_ref, z_ref, ut_sc, ht_sc, yt_sc = refs[CHUNK:]
    _chunk_operands(h_refs, ut_sc, ht_sc)
    xf = jnp.swapaxes(xf_ref[...].astype(F32), 0, 1)
    xb = jnp.swapaxes(xb_ref[...].astype(F32), 0, 1)
    for g in range(GROUP_BLOCK):
        xin = jnp.concatenate([xf[g], xb[g]], axis=1).astype(BF16)
        yt_sc[g] = _dot(mt_ref[g], ut_sc[g]) + _dot_nt(qt_ref[g], xin)
    d = d_ref[...]
    for t in range(CHUNK):
        y = yt_sc[:, SSM_GROUP * t:SSM_GROUP * (t + 1), :].reshape(GB_W, -1)
        v = y + d * ht_sc[t]
        z_ref[t] = jax.nn.gelu(v, approximate=True).T.astype(BF16)


def _s5_out_call(h_ph, xin_f, xin_b, mt, qt, d_col, nk):
    nc = h_ph.shape[0]
    xspec = pl.BlockSpec((nk, GROUP_BLOCK, SW), lambda gb, i: (i, gb, 0))
    wspec = pl.BlockSpec((GROUP_BLOCK, CW, CW), lambda gb, i: (gb, 0, 0))
    return pl.pallas_call(
        _s5_out_kernel,
        grid=(N_GB, nc // nk),
        in_specs=[*_h_specs(nk), xspec, xspec, wspec, wspec,
                  pl.BlockSpec((GB_W, nk), lambda gb, i: (gb, 0))],
        out_specs=pl.BlockSpec((CHUNK, nk, GB_W), lambda gb, i: (0, i, gb)),
        out_shape=jax.ShapeDtypeStruct((CHUNK, nc, D_MODEL), BF16),
        scratch_shapes=[
            pltpu.VMEM((GROUP_BLOCK, CW, nk), BF16),
            pltpu.VMEM((CHUNK, GB_W, nk), F32),
            pltpu.VMEM((GROUP_BLOCK, CW, nk), F32),
        ],
        compiler_params=_cparams(("parallel", "parallel")),
        name="s5_outputs",
    )(*([h_ph] * CHUNK), xin_f, xin_b, mt, qt, d_col)


def _s5_matrices(lam_re, lam_im, log_step, b_re, b_im, c_re, c_im):
    hp = lax.Precision.HIGHEST
    dt = jnp.exp(log_step.astype(F32))[..., None]
    lam_re = lam_re.astype(F32)
    lam_im = lam_im.astype(F32)
    mag = jnp.exp(lam_re * dt)
    ang = lam_im * dt
    lb_re = mag * jnp.cos(ang)
    lb_im = mag * jnp.sin(ang)
    den = lam_re * lam_re + lam_im * lam_im
    nr = lb_re - 1.0
    ni = lb_im
    coef_re = (nr * lam_re + ni * lam_im) / den
    coef_im = (ni * lam_re - nr * lam_im) / den
    b_re = b_re.astype(F32)
    b_im = b_im.astype(F32)
    bb_re = coef_re[..., None] * b_re - coef_im[..., None] * b_im
    bb_im = coef_re[..., None] * b_im + coef_im[..., None] * b_re
    c_re = c_re.astype(F32)
    c_im = c_im.astype(F32)

    pw_re = [jnp.ones_like(lb_re)]
    pw_im = [jnp.zeros_like(lb_im)]
    for _ in range(CHUNK):
        pr, pi = pw_re[-1], pw_im[-1]
        pw_re.append(pr * lb_re - pi * lb_im)
        pw_im.append(pr * lb_im + pi * lb_re)
    pw_re = jnp.stack(pw_re, axis=-1)
    pw_im = jnp.stack(pw_im, axis=-1)

    bbl_re = jnp.tile(bb_re, (1, 1, 1, CHUNK))
    bbl_im = jnp.tile(bb_im, (1, 1, 1, CHUNK))

    def response(pr, pi):
        prl = jnp.repeat(pr, SSM_GROUP, axis=-1)
        pil = jnp.repeat(pi, SSM_GROUP, axis=-1)
        return prl * bbl_re - pil * bbl_im, prl * bbl_im + pil * bbl_re

    w_re, w_im = response(pw_re[..., :CHUNK], pw_im[..., :CHUNK])
    wr_re, wr_im = response(pw_re[..., CHUNK - 1::-1], pw_im[..., CHUNK - 1::-1])

    kmat = (jnp.einsum('dgcn,dgnx->dgcx', c_re, w_re, precision=hp)
            - jnp.einsum('dgcn,dgnx->dgcx', c_im, w_im, precision=hp))
    lane = jnp.arange(CW)
    step, chan = lane // SSM_GROUP, lane % SSM_GROUP
    same_c = chan[:, None] == chan[None, :]
    t_i = jnp.arange(CHUNK)[:, None, None]
    sel_f = (same_c[None] & (step[None, :, None] == t_i - step[None, None, :])).astype(F32)
    sel_b = (same_c[None] & (step[None, :, None] == step[None, None, :] - t_i)).astype(F32)
    mt = (jnp.einsum('gcx,txy->gtcy', kmat[0], sel_f, precision=hp)
          + jnp.einsum('gcx,txy->gtcy', kmat[1], sel_b, precision=hp)).reshape(N_GROUPS, CW, CW)

    pt = jnp.stack([wr_re[0], wr_im[0], w_re[1], w_im[1]], axis=1).reshape(N_GROUPS, 2 * SW, CW)

    def carry(d, pr, pi):
        cr = c_re[d][:, None]
        ci = c_im[d][:, None]
        pr = pr.transpose(0, 2, 1)[:, :, None, :]
        pi = pi.transpose(0, 2, 1)[:, :, None, :]
        return [cr * pr - ci * pi, -(cr * pi + ci * pr)]

    qt = jnp.concatenate(carry(0, pw_re[0, ..., 1:], pw_im[0, ..., 1:])
                         + carry(1, pw_re[1, ..., :0:-1], pw_im[1, ..., :0:-1]), axis=-1)
    qt = qt.reshape(N_GROUPS, CW, 2 * SW)

    a_re = pw_re[..., CHUNK]
    a_im = pw_im[..., CHUNK]
    ar = jnp.concatenate([a_re, a_re], axis=-1)
    ai = jnp.concatenate([-a_im, a_im], axis=-1)
    return mt.astype(BF16), pt.astype(BF16), qt.astype(BF16), ar, ai


def _trunk(x, p, tm, nk):
    b, l, _ = x.shape
    nc = b * l // CHUNK
    kc = min(256, l // CHUNK)
    q_t, k, v_t = _qkv_call(x, p['g_mix0'], p['wqkv_t'], p['gq'], p['gk'], tm)
    o_t = _attn_call(q_t, k, v_t, p['hm'], p['bias'], p['rmask'])
    x_ph, h_ph = _attn_ffn_call(x, o_t, p['wo'], p['g_ffn0'], p['wgu0'], p['wd0'], p['g_mix1'], tm)

    sf, sb = _s5_state_call(h_ph, p['pt'], nk)
    sshape = (b, l // CHUNK, N_GROUPS, SW)
    xin_f = _scan_call(sf.reshape(sshape), p['ar'][0], p['ai'][0], kc, False)
    xin_b = _scan_call(sb.reshape(sshape), p['ar'][1], p['ai'][1], kc, True)
    z_pm = _s5_out_call(h_ph, xin_f.reshape(nc, N_GROUPS, SW), xin_b.reshape(nc, N_GROUPS, SW),
                        p['mt'], p['qt'], p['d_col'], nk)
    y = _glu_ffn_call(x_ph, z_pm, p['wglu'], p['g_ffn1'], p['wgu1'], p['wd1'], tm)
    return y.reshape(b, l, D_MODEL)


def kernel(x_prompt, x_sample, norm_mix, norm_ffn, w_qkv, w_o, q_gain, k_gain, rpb, lam_re, lam_im, log_step,
           b_re, b_im, c_re, c_im, d_skip, w_glu, w_gate_up, w_down):
    tm, nk = 512, 128
    scale = HEAD_DIM ** -0.5 * LOG2E
    mt, pt, qt, ar, ai = _s5_matrices(lam_re[0], lam_im[0], log_step[0], b_re[0], b_im[0], c_re[0], c_im[0])
    bias, rmask = _attn_tables(rpb[0])
    rows = np.arange(PAIR_W)[None, :, None]
    hm = np.broadcast_to((rows // HEAD_DIM) == np.arange(2)[:, None, None], (2, PAIR_W, QB))
    p = dict(
        g_mix0=norm_mix[0].astype(F32)[None], g_mix1=norm_mix[1].astype(F32)[None],
        g_ffn0=norm_ffn[0].astype(F32)[None], g_ffn1=norm_ffn[1].astype(F32)[None],
        wqkv_t=w_qkv[0].T.astype(BF16), wo=w_o[0].astype(BF16),
        gq=jnp.broadcast_to((q_gain[0].astype(F32) * scale)[:, None], (HEAD_DIM, tm // ROW_SPLIT)),
        gk=jnp.broadcast_to(k_gain[0].astype(F32)[:, None], (HEAD_DIM, tm // ROW_SPLIT)),
        hm=jnp.asarray(hm, BF16), bias=bias, rmask=rmask,
        wgu0=w_gate_up[0].astype(BF16), wd0=w_down[0].astype(BF16),
        wgu1=w_gate_up[1].astype(BF16), wd1=w_down[1].astype(BF16),
        wglu=w_glu[0].astype(BF16),
        mt=mt, pt=pt, qt=qt, ar=ar, ai=ai,
        d_col=jnp.broadcast_to(d_skip[0].astype(F32)[:, None], (D_MODEL, nk)),
    )
    return _trunk(x_prompt, p, tm, nk), _trunk(x_sample, p, tm, nk)
```

```python
import functools

import numpy as np
import jax
import jax.numpy as jnp
from jax import lax
from jax.experimental import pallas as pl
from jax.experimental.pallas import tpu as pltpu

D_MODEL = 1024
GRID_W = 64
N_HEADS = 16
HEAD_DIM = D_MODEL // N_HEADS
WIN_ROWS = 8
WIN_COLS = 16
SSM_GROUP = 16
N_GROUPS = D_MODEL // SSM_GROUP
STATE = 64
D_FF = 2816
EPS = 1e-6

F32 = jnp.float32
BF16 = jnp.bfloat16

V7X_LANES = 128
V7X_VMEM_BYTES = 64 * 1024 * 1024
VMEM_LIMIT = (V7X_VMEM_BYTES * 7) // 8

HEAD_PAIRS = N_HEADS // 2
PAIR_W = 2 * HEAD_DIM
Q_ROWS = 4
QB = Q_ROWS * GRID_W
KB = 3 * QB
NEG = -1e30
ONES_ROWS = 16
LOG2E = 1.4426950408889634
RM_ROWS = 16
TILE_SKIP, TILE_OPEN, TILE_MASKED = 0, 1, 2


def _tile_plans():
    assert 2 * Q_ROWS <= WIN_ROWS and V7X_LANES % GRID_W == 0
    per_tile = V7X_LANES // GRID_W
    interior, edge = [], []
    for kr in range(3 * Q_ROWS):
        irow, erow = [], []
        for c in range(QB // V7X_LANES):
            ok = [r <= kr < r + WIN_ROWS for r in range(per_tile * c, per_tile * (c + 1))]
            irow.append(TILE_OPEN if all(ok) else TILE_MASKED if any(ok) else TILE_SKIP)
            erow.append(TILE_OPEN if Q_ROWS <= kr < 2 * Q_ROWS else TILE_MASKED)
        interior.append(irow)
        edge.append(erow)
    return interior, edge


_INTERIOR_PLAN, _EDGE_PLAN = _tile_plans()

CHUNK = 16
CW = CHUNK * SSM_GROUP
SW = 2 * STATE
GROUP_BLOCK = 16
GB_W = GROUP_BLOCK * SSM_GROUP
N_GB = N_GROUPS // GROUP_BLOCK
ROW_SPLIT = 2


def _cparams(sem):
    return pltpu.CompilerParams(dimension_semantics=sem, vmem_limit_bytes=VMEM_LIMIT)


def _rms(x, g):
    return x * lax.rsqrt(jnp.mean(x * x, axis=-1, keepdims=True) + EPS) * g


def _dot(a, b):
    return jnp.dot(a, b, preferred_element_type=F32)


def _dot_nt(a, b):
    return lax.dot_general(a, b, (((1,), (1,)), ((), ())), preferred_element_type=F32)


def _dot_tn(a, b):
    return lax.dot_general(a, b, (((0,), (0,)), ((), ())), preferred_element_type=F32)


def _qkv_kernel(x_ref, g_ref, w_ref, gq_ref, gk_ref, q_ref, k_ref, v_ref):
    tm = x_ref.shape[1] // ROW_SPLIT

    def head_norm(t, gain):
        t3 = t.reshape(N_HEADS, HEAD_DIM, tm)
        y = t3 * lax.rsqrt(jnp.mean(t3 * t3, axis=1, keepdims=True) + EPS)
        return y * gain[None]

    for r in range(ROW_SPLIT):
        tok = slice(r * tm, (r + 1) * tm)
        h = _rms(x_ref[0, tok, :], g_ref[...]).astype(BF16)
        qkv = _dot_nt(w_ref[...], h)
        q = head_norm(qkv[0:D_MODEL], gq_ref[...])
        k = head_norm(qkv[D_MODEL:2 * D_MODEL], gk_ref[...])
        v = qkv[2 * D_MODEL:3 * D_MODEL]
        q_ref[0, r] = q.reshape(HEAD_PAIRS, PAIR_W, tm).astype(BF16)
        v_ref[0, r] = v.reshape(HEAD_PAIRS, PAIR_W, tm).astype(BF16)
        kt = k.reshape(D_MODEL, tm).T
        for j in range(HEAD_PAIRS):
            k_ref[0, j, tok, :] = kt[:, PAIR_W * j:PAIR_W * (j + 1)].astype(BF16)


def _qkv_call(x, g, w_t, gq, gk, tm):
    b, l, _ = x.shape
    assert tm // ROW_SPLIT == QB
    blocked = pl.BlockSpec((1, ROW_SPLIT, HEAD_PAIRS, PAIR_W, QB), lambda i, j: (i, j, 0, 0, 0))
    return pl.pallas_call(
        _qkv_kernel,
        grid=(b, l // tm),
        in_specs=[
            pl.BlockSpec((1, tm, D_MODEL), lambda i, j: (i, j, 0)),
            pl.BlockSpec((1, D_MODEL), lambda i, j: (0, 0)),
            pl.BlockSpec((3 * D_MODEL, D_MODEL), lambda i, j: (0, 0)),
            pl.BlockSpec((HEAD_DIM, tm // ROW_SPLIT), lambda i, j: (0, 0)),
            pl.BlockSpec((HEAD_DIM, tm // ROW_SPLIT), lambda i, j: (0, 0)),
        ],
        out_specs=[
            blocked,
            pl.BlockSpec((1, HEAD_PAIRS, tm, PAIR_W), lambda i, j: (i, 0, j, 0)),
            blocked,
        ],
        out_shape=[
            jax.ShapeDtypeStruct((b, l // QB, HEAD_PAIRS, PAIR_W, QB), BF16),
            jax.ShapeDtypeStruct((b, HEAD_PAIRS, l, PAIR_W), BF16),
            jax.ShapeDtypeStruct((b, l // QB, HEAD_PAIRS, PAIR_W, QB), BF16),
        ],
        compiler_params=_cparams(("parallel", "parallel")),
        name="qkv_proj",
    )(x, g, w_t, gq, gk)


def _attn_kernel(q_ref, kp_ref, kc_ref, kn_ref, vp_ref, vc_ref, vn_ref, hm_ref, bias_ref, rm_ref, o_ref,
                 sa_sc, sb_sc, ma_sc, mb_sc):
    ones = jnp.ones((ONES_ROWS, QB), BF16)
    zero_tile = jnp.zeros((GRID_W, V7X_LANES), BF16)
    lane_tiles = QB // V7X_LANES

    def tile(kr, c):
        return slice(GRID_W * kr, GRID_W * (kr + 1)), slice(V7X_LANES * c, V7X_LANES * (c + 1))

    def scores(j, s_sc, m_sc, plan):
        q2 = q_ref[0, 0, j]
        for e in range(2):
            qm = q2 * hm_ref[e]
            mx = [None] * lane_tiles
            for blk, k_ref in enumerate((kp_ref, kc_ref, kn_ref)):
                d = _dot(k_ref[0, j], qm)
                for kl in range(Q_ROWS):
                    kr = Q_ROWS * blk + kl
                    for c in range(lane_tiles):
                        if plan[kr][c] == TILE_SKIP:
                            continue
                        rows, lanes = tile(kr, c)
                        s = d[GRID_W * kl:GRID_W * (kl + 1), lanes] + bias_ref[j, e, rows, lanes]
                        if plan[kr][c] == TILE_MASKED:
                            s = s + rm_ref[0, kr:kr + 1, lanes]
                        s_sc[e, rows, lanes] = s
                        mx[c] = s if mx[c] is None else jnp.maximum(mx[c], s)
            m_sc[e] = jnp.concatenate(mx, axis=1)

    def finish(j, s_sc, m_sc, plan):
        for e in range(2):
            lo, hi = HEAD_DIM * e, HEAD_DIM * (e + 1)
            m = jnp.max(m_sc[e], axis=0, keepdims=True)
            o = None
            for blk, v_ref in enumerate((vp_ref, vc_ref, vn_ref)):
                p_rows = []
                for kl in range(Q_ROWS):
                    kr = Q_ROWS * blk + kl
                    p_tiles = []
                    for c in range(lane_tiles):
                        rows, lanes = tile(kr, c)
                        p_tiles.append(zero_tile if plan[kr][c] == TILE_SKIP else
                                       jnp.exp2(s_sc[e, rows, lanes] - m[:, lanes]).astype(BF16))
                    p_rows.append(jnp.concatenate(p_tiles, axis=1))
                pb = jnp.concatenate(p_rows, axis=0)
                v = jnp.concatenate([v_ref[0, 0, j, lo:hi, :], ones], axis=0)
                ob = _dot(v, pb)
                o = ob if o is None else o + ob
            o_ref[0, 0, j, lo:hi, :] = (o[:HEAD_DIM] / o[HEAD_DIM:HEAD_DIM + 1]).astype(BF16)

    def all_heads(plan):
        scores(0, sa_sc, ma_sc, plan)

        def body(i, carry):
            j = 2 * i
            scores(j + 1, sb_sc, mb_sc, plan)
            finish(j, sa_sc, ma_sc, plan)
            scores(j + 2, sa_sc, ma_sc, plan)
            finish(j + 1, sb_sc, mb_sc, plan)
            return carry

        lax.fori_loop(0, HEAD_PAIRS // 2 - 1, body, 0)
        scores(HEAD_PAIRS - 1, sb_sc, mb_sc, plan)
        finish(HEAD_PAIRS - 2, sa_sc, ma_sc, plan)
        finish(HEAD_PAIRS - 1, sb_sc, mb_sc, plan)

    interior = jnp.logical_and(pl.program_id(1) > 0, pl.program_id(1) < pl.num_programs(1) - 1)
    pl.when(interior)(lambda: all_heads(_INTERIOR_PLAN))
    pl.when(jnp.logical_not(interior))(lambda: all_heads(_EDGE_PLAN))


def _attn_call(q_t, k, v_t, hm, bias, rmask):
    b, nblk = q_t.shape[:2]
    prev = lambda j: jnp.maximum(j - 1, 0)
    nxt = lambda j: jnp.minimum(j + 1, nblk - 1)
    variant = lambda j: (j > 0).astype(jnp.int32) + (j == nblk - 1).astype(jnp.int32)
    kspec = lambda f: pl.BlockSpec((1, HEAD_PAIRS, QB, PAIR_W), lambda i, j: (i, 0, f(j), 0))
    vspec = lambda f: pl.BlockSpec((1, 1, HEAD_PAIRS, PAIR_W, QB), lambda i, j: (i, f(j), 0, 0, 0))
    same = lambda j: j
    return pl.pallas_call(
        _attn_kernel,
        grid=(b, nblk),
        in_specs=[
            vspec(same),
            kspec(prev), kspec(same), kspec(nxt),
            vspec(prev), vspec(same), vspec(nxt),
            pl.BlockSpec((2, PAIR_W, QB), lambda i, j: (0, 0, 0)),
            pl.BlockSpec((HEAD_PAIRS, 2, KB, QB), lambda i, j: (0, 0, 0, 0), pipeline_mode=pl.Buffered(1)),
            pl.BlockSpec((1, RM_ROWS, QB), lambda i, j: (variant(j), 0, 0)),
        ],
        out_specs=vspec(same),
        out_shape=jax.ShapeDtypeStruct(q_t.shape, BF16),
        scratch_shapes=[pltpu.VMEM((2, KB, QB), F32)] * 2 + [pltpu.VMEM((2, GRID_W, QB), F32)] * 2,
        compiler_params=_cparams(("parallel", "parallel")),
        name="nbr_attention",
    )(q_t, k, k, k, v_t, v_t, v_t, hm, bias, rmask)


def _attn_tables(rpb):
    kr = np.arange(3 * Q_ROWS)[:, None, None, None]
    kc = np.arange(GRID_W)[None, :, None, None]
    r = np.arange(Q_ROWS)[None, None, :, None]
    c = np.arange(GRID_W)[None, None, None, :]
    shape = (3 * Q_ROWS, GRID_W, Q_ROWS, GRID_W)
    cs = np.clip(c - WIN_COLS // 2, 0, GRID_W - WIN_COLS)
    col_ok = ((kc >= cs) & (kc < cs + WIN_COLS))[0, :, 0, :]
    dc = (kc - c + WIN_COLS - 1)[0, :, 0, :]
    onehot = ((dc[None] == np.arange(2 * WIN_COLS - 1)[:, None, None]) & col_ok[None]).astype(np.float32)
    toep = jnp.einsum('hrd,dkc->hrkc', rpb.astype(F32), onehot, precision=lax.Precision.HIGHEST)
    toep = jnp.where(col_ok[None, None], toep * LOG2E, NEG)
    off = WIN_ROWS - 1 - Q_ROWS
    bias = jnp.concatenate([toep[:, off - rr:off - rr + 3 * Q_ROWS] for rr in range(Q_ROWS)], axis=-1)
    bias = bias.reshape(HEAD_PAIRS, 2, KB, QB)
    rshape = (3 * Q_ROWS, 1, Q_ROWS, GRID_W)
    first = np.broadcast_to(kr >= Q_ROWS, rshape)
    inner = np.broadcast_to((kr >= r) & (kr < r + WIN_ROWS), rshape)
    last = np.broadcast_to(kr < 2 * Q_ROWS, rshape)
    rmask = np.where(np.stack([first, inner, last]), 0.0, NEG).astype(np.float32).reshape(3, 3 * Q_ROWS, QB)
    rmask = np.pad(rmask, ((0, 0), (0, RM_ROWS - 3 * Q_ROWS), (0, 0)))
    return bias, jnp.asarray(rmask)


def _ffn(x0, gf_ref, wgu_ref, wd_ref):
    hb = _rms(x0, gf_ref[...]).astype(BF16)
    gu = _dot(hb, wgu_ref[...])
    g, u = gu[:, :D_FF], gu[:, D_FF:]
    act = (g * (1.0 / (1.0 + jnp.exp(-g))) * u).astype(BF16)
    return x0 + _dot(act, wd_ref[...])


def _resident(shape):
    return pl.BlockSpec(shape, lambda *idx: (0,) * len(shape), pipeline_mode=pl.Buffered(1))


def _phase_major(x):
    return jnp.swapaxes(x.reshape(-1, CHUNK, D_MODEL), 0, 1)


def _store_chunk_major(ref, x):
    xs = _phase_major(x)
    for s in range(CHUNK):
        ref[:, D_MODEL * s:D_MODEL * (s + 1)] = xs[s]


def _attn_ffn_kernel(x_ref, o_ref, wo_ref, gf_ref, wgu_ref, wd_ref, gn_ref, x1_ref, h1_ref):
    parts = []
    for r in range(ROW_SPLIT):
        o = o_ref[0, r].reshape(D_MODEL, QB)
        x0 = x_ref[0, QB * r:QB * (r + 1), :] + _dot_tn(o, wo_ref[...])
        parts.append(_ffn(x0, gf_ref, wgu_ref, wd_ref))
    x1 = jnp.concatenate(parts, axis=0)
    _store_chunk_major(x1_ref, x1)
    _store_chunk_major(h1_ref, _rms(x1, gn_ref[...]))


def _attn_ffn_call(x, o_t, wo, gf, wgu, wd, gn, tm):
    b, l, _ = x.shape
    nt = l // tm
    out = pl.BlockSpec((tm // CHUNK, CHUNK * D_MODEL), lambda i, j: (i * nt + j, 0))
    return pl.pallas_call(
        _attn_ffn_kernel,
        grid=(b, nt),
        in_specs=[
            pl.BlockSpec((1, tm, D_MODEL), lambda i, j: (i, j, 0)),
            pl.BlockSpec((1, ROW_SPLIT, HEAD_PAIRS, PAIR_W, QB), lambda i, j: (i, j, 0, 0, 0)),
            _resident((D_MODEL, D_MODEL)),
            _resident((1, D_MODEL)),
            _resident((D_MODEL, 2 * D_FF)),
            _resident((D_FF, D_MODEL)),
            _resident((1, D_MODEL)),
        ],
        out_specs=[out, out],
        out_shape=[jax.ShapeDtypeStruct((b * l // CHUNK, CHUNK * D_MODEL), F32)] * 2,
        compiler_params=_cparams(("parallel", "parallel")),
        name="attn_out_ffn",
    )(x, o_t, wo, gf, wgu, wd, gn)


def _glu_ffn_kernel(x_ref, z_ref, wglu_ref, gf_ref, wgu_ref, wd_ref, y_ref):
    parts = []
    for r in range(ROW_SPLIT):
        ph = range(r * CHUNK // ROW_SPLIT, (r + 1) * CHUNK // ROW_SPLIT)
        z = z_ref[ph.start:ph.stop].reshape(-1, D_MODEL)
        x = jnp.concatenate([x_ref[:, D_MODEL * s:D_MODEL * (s + 1)] for s in ph], axis=0)
        zz = _dot(z, wglu_ref[...])
        za, zg = zz[:, :D_MODEL], zz[:, D_MODEL:]
        x0 = x + za * (1.0 / (1.0 + jnp.exp(-zg)))
        parts.append(_ffn(x0, gf_ref, wgu_ref, wd_ref))
    y = jnp.concatenate(parts, axis=0).reshape(CHUNK, -1, D_MODEL)
    y_ref[...] = jnp.swapaxes(y, 0, 1).reshape(-1, D_MODEL)


def _glu_ffn_call(x_ph, z_pm, wglu, gf, wgu, wd, tm):
    nc = x_ph.shape[0]
    nk = tm // CHUNK
    return pl.pallas_call(
        _glu_ffn_kernel,
        grid=(nc // nk,),
        in_specs=[
            pl.BlockSpec((nk, CHUNK * D_MODEL), lambda i: (i, 0)),
            pl.BlockSpec((CHUNK, nk, D_MODEL), lambda i: (0, i, 0)),
            _resident((D_MODEL, 2 * D_MODEL)),
            _resident((1, D_MODEL)),
            _resident((D_MODEL, 2 * D_FF)),
            _resident((D_FF, D_MODEL)),
        ],
        out_specs=pl.BlockSpec((tm, D_MODEL), lambda i: (i, 0)),
        out_shape=jax.ShapeDtypeStruct((nc * CHUNK, D_MODEL), F32),
        compiler_params=_cparams(("parallel",)),
        name="glu_out_ffn",
    )(x_ph, z_pm, wglu, gf, wgu, wd)


def _chunk_operands(h_refs, ut_sc, ht_sc):
    for s, h_ref in enumerate(h_refs):
        ht = h_ref[...].T
        if ht_sc is not None:
            ht_sc[s] = ht
        ut_sc[:, SSM_GROUP * s:SSM_GROUP * (s + 1), :] = (
            ht.reshape(GROUP_BLOCK, SSM_GROUP, -1).astype(BF16))


def _s5_state_kernel(*refs):
    h_refs = refs[:CHUNK]
    pt_ref, sf_ref, sb_ref, ut_sc, st_sc = refs[CHUNK:]
    _chunk_operands(h_refs, ut_sc, None)
    for g in range(GROUP_BLOCK):
        st_sc[g] = _dot(pt_ref[g], ut_sc[g]).T
    sf_ref[...] = jnp.swapaxes(st_sc[:, :, :SW], 0, 1)
    sb_ref[...] = jnp.swapaxes(st_sc[:, :, SW:], 0, 1)


def _h_specs(nk):
    return [pl.BlockSpec((nk, GB_W), lambda gb, i, s=s: (i, s * N_GB + gb)) for s in range(CHUNK)]


def _s5_state_call(h_ph, pt, nk):
    nc = h_ph.shape[0]
    sspec = pl.BlockSpec((nk, GROUP_BLOCK, SW), lambda gb, i: (i, gb, 0))
    return pl.pallas_call(
        _s5_state_kernel,
        grid=(N_GB, nc // nk),
        in_specs=[*_h_specs(nk),
                  pl.BlockSpec((GROUP_BLOCK, 2 * SW, CW), lambda gb, i: (gb, 0, 0))],
        out_specs=[sspec, sspec],
        out_shape=[jax.ShapeDtypeStruct((nc, N_GROUPS, SW), F32)] * 2,
        scratch_shapes=[pltpu.VMEM((GROUP_BLOCK, CW, nk), BF16),
                        pltpu.VMEM((GROUP_BLOCK, nk, 2 * SW), F32)],
        compiler_params=_cparams(("parallel", "parallel")),
        name="s5_chunk_states",
    )(*([h_ph] * CHUNK), pt)


def _scan_kernel(sf_ref, sb_ref, ar_ref, ai_ref, xf_ref, xb_ref, st_sc):
    kc = sf_ref.shape[1]

    @pl.when(pl.program_id(1) == 0)
    def _():
        st_sc[...] = jnp.zeros_like(st_sc)

    a = [(ar_ref[d], ai_ref[d]) for d in range(2)]

    def step(d, s_ref, x_ref, k, x, xs):
        ar, ai = a[d]
        s = s_ref[0, k]
        x_ref[0, k] = x.astype(BF16)
        return ar * x + ai * xs + s, ar * xs - ai * x + pltpu.roll(s, STATE, 1)

    def body(i, carry):
        xf, xfs, xb, xbs = carry
        return (*step(0, sf_ref, xf_ref, i, xf, xfs), *step(1, sb_ref, xb_ref, kc - 1 - i, xb, xbs))

    init = (st_sc[0, 0], st_sc[0, 1], st_sc[1, 0], st_sc[1, 1])
    xf, xfs, xb, xbs = lax.fori_loop(0, kc, body, init, unroll=8)
    st_sc[0, 0], st_sc[0, 1], st_sc[1, 0], st_sc[1, 1] = xf, xfs, xb, xbs


def _scan_call(sf, sb, ar, ai, kc):
    b, ncs = sf.shape[:2]
    nkb = ncs // kc
    fwd = pl.BlockSpec((1, kc, N_GROUPS, SW), lambda i, j: (i, j, 0, 0))
    bwd = pl.BlockSpec((1, kc, N_GROUPS, SW), lambda i, j: (i, nkb - 1 - j, 0, 0))
    tab = pl.BlockSpec((2, N_GROUPS, SW), lambda i, j: (0, 0, 0))
    return pl.pallas_call(
        _scan_kernel,
        grid=(b, nkb),
        in_specs=[fwd, bwd, tab, tab],
        out_specs=[fwd, bwd],
        out_shape=[jax.ShapeDtypeStruct(sf.shape, BF16)] * 2,
        scratch_shapes=[pltpu.VMEM((2, 2, N_GROUPS, SW), F32)],
        compiler_params=_cparams(("parallel", "arbitrary")),
        name="s5_chunk_scan",
    )(sf, sb, ar, ai)


def _s5_out_kernel(*refs):
    h_refs = refs[:CHUNK]
    xf_ref, xb_ref, mt_ref, qt_ref, d_ref, z_ref, ut_sc, ht_sc, yt_sc = refs[CHUNK:]
    _chunk_operands(h_refs, ut_sc, ht_sc)
    xf = jnp.swapaxes(xf_ref[...].astype(F32), 0, 1)
    xb = jnp.swapaxes(xb_ref[...].astype(F32), 0, 1)
    for g in range(GROUP_BLOCK):
        xin = jnp.concatenate([xf[g], xb[g]], axis=1).astype(BF16)
        yt_sc[g] = _dot(mt_ref[g], ut_sc[g]) + _dot_nt(qt_ref[g], xin)
    d = d_ref[...]
    for t in range(CHUNK):
        y = yt_sc[:, SSM_GROUP * t:SSM_GROUP * (t + 1), :].reshape(GB_W, -1)
        v = y + d * ht_sc[t]
        z_ref[t] = jax.nn.gelu(v, approximate=True).T.astype(BF16)


def _s5_out_call(h_ph, xin_f, xin_b, mt, qt, d_col, nk):
    nc = h_ph.shape[0]
    xspec = pl.BlockSpec((nk, GROUP_BLOCK, SW), lambda gb, i: (i, gb, 0))
    wspec = pl.BlockSpec((GROUP_BLOCK, CW, CW), lambda gb, i: (gb, 0, 0))
    return pl.pallas_call(
        _s5_out_kernel,
        grid=(N_GB, nc // nk),
        in_specs=[*_h_specs(nk), xspec, xspec, wspec, wspec,
                  pl.BlockSpec((GB_W, nk), lambda gb, i: (gb, 0))],
        out_specs=pl.BlockSpec((CHUNK, nk, GB_W), lambda gb, i: (0, i, gb)),
        out_shape=jax.ShapeDtypeStruct((CHUNK, nc, D_MODEL), BF16),
        scratch_shapes=[
            pltpu.VMEM((GROUP_BLOCK, CW, nk), BF16),
            pltpu.VMEM((CHUNK, GB_W, nk), F32),
            pltpu.VMEM((GROUP_BLOCK, CW, nk), F32),
        ],
        compiler_params=_cparams(("parallel", "parallel")),
        name="s5_outputs",
    )(*([h_ph] * CHUNK), xin_f, xin_b, mt, qt, d_col)


def _s5_matrices(lam_re, lam_im, log_step, b_re, b_im, c_re, c_im):
    hp = lax.Precision.HIGHEST
    dt = jnp.exp(log_step.astype(F32))[..., None]
    lam_re = lam_re.astype(F32)
    lam_im = lam_im.astype(F32)
    mag = jnp.exp(lam_re * dt)
    ang = lam_im * dt
    lb_re = mag * jnp.cos(ang)
    lb_im = mag * jnp.sin(ang)
    den = lam_re * lam_re + lam_im * lam_im
    nr = lb_re - 1.0
    ni = lb_im
    coef_re = (nr * lam_re + ni * lam_im) / den
    coef_im = (ni * lam_re - nr * lam_im) / den
    b_re = b_re.astype(F32)
    b_im = b_im.astype(F32)
    bb_re = coef_re[..., None] * b_re - coef_im[..., None] * b_im
    bb_im = coef_re[..., None] * b_im + coef_im[..., None] * b_re
    c_re = c_re.astype(F32)
    c_im = c_im.astype(F32)

    pw_re = [jnp.ones_like(lb_re)]
    pw_im = [jnp.zeros_like(lb_im)]
    for _ in range(CHUNK):
        pr, pi = pw_re[-1], pw_im[-1]
        pw_re.append(pr * lb_re - pi * lb_im)
        pw_im.append(pr * lb_im + pi * lb_re)
    pw_re = jnp.stack(pw_re, axis=-1)
    pw_im = jnp.stack(pw_im, axis=-1)

    bbl_re = jnp.tile(bb_re, (1, 1, 1, CHUNK))
    bbl_im = jnp.tile(bb_im, (1, 1, 1, CHUNK))

    def response(pr, pi):
        prl = jnp.repeat(pr, SSM_GROUP, axis=-1)
        pil = jnp.repeat(pi, SSM_GROUP, axis=-1)
        return prl * bbl_re - pil * bbl_im, prl * bbl_im + pil * bbl_re

    w_re, w_im = response(pw_re[..., :CHUNK], pw_im[..., :CHUNK])
    wr_re, wr_im = response(pw_re[..., CHUNK - 1::-1], pw_im[..., CHUNK - 1::-1])

    kmat = (jnp.einsum('dgcn,dgnx->dgcx', c_re, w_re, precision=hp)
            - jnp.einsum('dgcn,dgnx->dgcx', c_im, w_im, precision=hp))
    lane = jnp.arange(CW)
    step, chan = lane // SSM_GROUP, lane % SSM_GROUP
    same_c = chan[:, None] == chan[None, :]
    t_i = jnp.arange(CHUNK)[:, None, None]
    sel_f = (same_c[None] & (step[None, :, None] == t_i - step[None, None, :])).astype(F32)
    sel_b = (same_c[None] & (step[None, :, None] == step[None, None, :] - t_i)).astype(F32)
    mt = (jnp.einsum('gcx,txy->gtcy', kmat[0], sel_f, precision=hp)
          + jnp.einsum('gcx,txy->gtcy', kmat[1], sel_b, precision=hp)).reshape(N_GROUPS, CW, CW)

    pt = jnp.stack([wr_re[0], wr_im[0], w_re[1], w_im[1]], axis=1).reshape(N_GROUPS, 2 * SW, CW)

    def carry(d, pr, pi):
        cr = c_re[d][:, None]
        ci = c_im[d][:, None]
        pr = pr.transpose(0, 2, 1)[:, :, None, :]
        pi = pi.transpose(0, 2, 1)[:, :, None, :]
        return [cr * pr - ci * pi, -(cr * pi + ci * pr)]

    qt = jnp.concatenate(carry(0, pw_re[0, ..., 1:], pw_im[0, ..., 1:])
                         + carry(1, pw_re[1, ..., :0:-1], pw_im[1, ..., :0:-1]), axis=-1)
    qt = qt.reshape(N_GROUPS, CW, 2 * SW)

    a_re = pw_re[..., CHUNK]
    a_im = pw_im[..., CHUNK]
    ar = jnp.concatenate([a_re, a_re], axis=-1)
    ai = jnp.concatenate([-a_im, a_im], axis=-1)
    return mt.astype(BF16), pt.astype(BF16), qt.astype(BF16), ar, ai


def _trunk(x, p, tm, nk):
    b, l, _ = x.shape
    nc = b * l // CHUNK
    kc = min(128, l // CHUNK)
    q_t, k, v_t = _qkv_call(x, p['g_mix0'], p['wqkv_t'], p['gq'], p['gk'], tm)
    o_t = _attn_call(q_t, k, v_t, p['hm'], p['bias'], p['rmask'])
    x_ph, h_ph = _attn_ffn_call(x, o_t, p['wo'], p['g_ffn0'], p['wgu0'], p['wd0'], p['g_mix1'], tm)

    sf, sb = _s5_state_call(h_ph, p['pt'], nk)
    sshape = (b, l // CHUNK, N_GROUPS, SW)
    xin_f, xin_b = _scan_call(sf.reshape(sshape), sb.reshape(sshape), p['ar'], p['ai'], kc)
    z_pm = _s5_out_call(h_ph, xin_f.reshape(nc, N_GROUPS, SW), xin_b.reshape(nc, N_GROUPS, SW),
                        p['mt'], p['qt'], p['d_col'], nk)
    y = _glu_ffn_call(x_ph, z_pm, p['wglu'], p['g_ffn1'], p['wgu1'], p['wd1'], tm)
    return y.reshape(b, l, D_MODEL)


def kernel(x_prompt, x_sample, norm_mix, norm_ffn, w_qkv, w_o, q_gain, k_gain, rpb, lam_re, lam_im, log_step,
           b_re, b_im, c_re, c_im, d_skip, w_glu, w_gate_up, w_down):
    tm, nk = 512, 128
    scale = HEAD_DIM ** -0.5 * LOG2E
    mt, pt, qt, ar, ai = _s5_matrices(lam_re[0], lam_im[0], log_step[0], b_re[0], b_im[0], c_re[0], c_im[0])
    bias, rmask = _attn_tables(rpb[0])
    rows = np.arange(PAIR_W)[None, :, None]
    hm = np.broadcast_to((rows // HEAD_DIM) == np.arange(2)[:, None, None], (2, PAIR_W, QB))
    p = dict(
        g_mix0=norm_mix[0].astype(F32)[None], g_mix1=norm_mix[1].astype(F32)[None],
        g_ffn0=norm_ffn[0].astype(F32)[None], g_ffn1=norm_ffn[1].astype(F32)[None],
        wqkv_t=w_qkv[0].T.astype(BF16), wo=w_o[0].astype(BF16),
        gq=jnp.broadcast_to((q_gain[0].astype(F32) * scale)[:, None], (HEAD_DIM, tm // ROW_SPLIT)),
        gk=jnp.broadcast_to(k_gain[0].astype(F32)[:, None], (HEAD_DIM, tm // ROW_SPLIT)),
        hm=jnp.asarray(hm, BF16), bias=bias, rmask=rmask,
        wgu0=w_gate_up[0].astype(BF16), wd0=w_down[0].astype(BF16),
        wgu1=w_gate_up[1].astype(BF16), wd1=w_down[1].astype(BF16),
        wglu=w_glu[0].astype(BF16),
        mt=mt, pt=pt, qt=qt, ar=ar, ai=ai,
        d_col=jnp.broadcast_to(d_skip[0].astype(F32)[:, None], (D_MODEL, nk)),
    )
    return _trunk(x_prompt, p, tm, nk), _trunk(x_sample, p, tm, nk)
```

```python
import functools

import numpy as np
import jax
import jax.numpy as jnp
from jax import lax
from jax.experimental import pallas as pl
from jax.experimental.pallas import tpu as pltpu

D_MODEL = 1024
GRID_W = 64
N_HEADS = 16
HEAD_DIM = D_MODEL // N_HEADS
WIN_ROWS = 8
WIN_COLS = 16
SSM_GROUP = 16
N_GROUPS = D_MODEL // SSM_GROUP
STATE = 64
D_FF = 2816
EPS = 1e-6

F32 = jnp.float32
BF16 = jnp.bfloat16

V7X_LANES = 128
V7X_VMEM_BYTES = 64 * 1024 * 1024
VMEM_LIMIT = (V7X_VMEM_BYTES * 7) // 8

HEAD_PAIRS = N_HEADS // 2
PAIR_W = 2 * HEAD_DIM
Q_ROWS = 4
QB = Q_ROWS * GRID_W
KB = 3 * QB
NEG = -1e30
ONES_ROWS = 16
LOG2E = 1.4426950408889634
RM_ROWS = 16
TILE_SKIP, TILE_OPEN, TILE_MASKED = 0, 1, 2
HEADS_PER_TRIP = 4


def _tile_plans():
    assert 2 * Q_ROWS <= WIN_ROWS and V7X_LANES % GRID_W == 0
    per_tile = V7X_LANES // GRID_W
    interior, edge = [], []
    for kr in range(3 * Q_ROWS):
        irow, erow = [], []
        for c in range(QB // V7X_LANES):
            ok = [r <= kr < r + WIN_ROWS for r in range(per_tile * c, per_tile * (c + 1))]
            irow.append(TILE_OPEN if all(ok) else TILE_MASKED if any(ok) else TILE_SKIP)
            erow.append(TILE_OPEN if Q_ROWS <= kr < 2 * Q_ROWS else TILE_MASKED)
        interior.append(irow)
        edge.append(erow)
    return interior, edge


_INTERIOR_PLAN, _EDGE_PLAN = _tile_plans()

CHUNK = 16
CW = CHUNK * SSM_GROUP
SW = 2 * STATE
GROUP_BLOCK = 16
GB_W = GROUP_BLOCK * SSM_GROUP
N_GB = N_GROUPS // GROUP_BLOCK
ROW_SPLIT = 2


def _cparams(sem):
    return pltpu.CompilerParams(dimension_semantics=sem, vmem_limit_bytes=VMEM_LIMIT)


def _rms(x, g):
    return x * lax.rsqrt(jnp.mean(x * x, axis=-1, keepdims=True) + EPS) * g


def _dot(a, b):
    return jnp.dot(a, b, preferred_element_type=F32)


def _dot_nt(a, b):
    return lax.dot_general(a, b, (((1,), (1,)), ((), ())), preferred_element_type=F32)


def _dot_tn(a, b):
    return lax.dot_general(a, b, (((0,), (0,)), ((), ())), preferred_element_type=F32)


def _qkv_kernel(x_ref, g_ref, w_ref, gq_ref, gk_ref, q_ref, k_ref, v_ref):
    tm = x_ref.shape[1] // ROW_SPLIT

    def head_norm(t, gain):
        t3 = t.reshape(N_HEADS, HEAD_DIM, tm)
        y = t3 * lax.rsqrt(jnp.mean(t3 * t3, axis=1, keepdims=True) + EPS)
        return y * gain[None]

    for r in range(ROW_SPLIT):
        tok = slice(r * tm, (r + 1) * tm)
        h = _rms(x_ref[0, tok, :], g_ref[...]).astype(BF16)
        qkv = _dot_nt(w_ref[...], h)
        q = head_norm(qkv[0:D_MODEL], gq_ref[...])
        k = head_norm(qkv[D_MODEL:2 * D_MODEL], gk_ref[...])
        v = qkv[2 * D_MODEL:3 * D_MODEL]
        q_ref[0, r] = q.reshape(HEAD_PAIRS, PAIR_W, tm).astype(BF16)
        v_ref[0, r] = v.reshape(HEAD_PAIRS, PAIR_W, tm).astype(BF16)
        kt = k.reshape(D_MODEL, tm).T
        for j in range(HEAD_PAIRS):
            k_ref[0, j, tok, :] = kt[:, PAIR_W * j:PAIR_W * (j + 1)].astype(BF16)


def _qkv_call(x, g, w_t, gq, gk, tm):
    b, l, _ = x.shape
    assert tm // ROW_SPLIT == QB
    blocked = pl.BlockSpec((1, ROW_SPLIT, HEAD_PAIRS, PAIR_W, QB), lambda i, j: (i, j, 0, 0, 0))
    return pl.pallas_call(
        _qkv_kernel,
        grid=(b, l // tm),
        in_specs=[
            pl.BlockSpec((1, tm, D_MODEL), lambda i, j: (i, j, 0)),
            pl.BlockSpec((1, D_MODEL), lambda i, j: (0, 0)),
            pl.BlockSpec((3 * D_MODEL, D_MODEL), lambda i, j: (0, 0)),
            pl.BlockSpec((HEAD_DIM, tm // ROW_SPLIT), lambda i, j: (0, 0)),
            pl.BlockSpec((HEAD_DIM, tm // ROW_SPLIT), lambda i, j: (0, 0)),
        ],
        out_specs=[
            blocked,
            pl.BlockSpec((1, HEAD_PAIRS, tm, PAIR_W), lambda i, j: (i, 0, j, 0)),
            blocked,
        ],
        out_shape=[
            jax.ShapeDtypeStruct((b, l // QB, HEAD_PAIRS, PAIR_W, QB), BF16),
            jax.ShapeDtypeStruct((b, HEAD_PAIRS, l, PAIR_W), BF16),
            jax.ShapeDtypeStruct((b, l // QB, HEAD_PAIRS, PAIR_W, QB), BF16),
        ],
        compiler_params=_cparams(("parallel", "parallel")),
        name="qkv_proj",
    )(x, g, w_t, gq, gk)


def _attn_kernel(q_ref, kp_ref, kc_ref, kn_ref, vp_ref, vc_ref, vn_ref, hm_ref, bias_ref, rm_ref, o_ref,
                 sa_sc, sb_sc, ma_sc, mb_sc):
    ones = jnp.ones((ONES_ROWS, QB), BF16)
    zero_tile = jnp.zeros((GRID_W, V7X_LANES), BF16)
    lane_tiles = QB // V7X_LANES

    def tile(kr, c):
        return slice(GRID_W * kr, GRID_W * (kr + 1)), slice(V7X_LANES * c, V7X_LANES * (c + 1))

    def scores(j, e, s_sc, m_sc, plan):
        qm = q_ref[0, 0, j] * hm_ref[e]
        mx = [None] * lane_tiles
        for blk, k_ref in enumerate((kp_ref, kc_ref, kn_ref)):
            used = [kl for kl in range(Q_ROWS) if any(t != TILE_SKIP for t in plan[Q_ROWS * blk + kl])]
            k0 = used[0]
            d = _dot(k_ref[0, j, GRID_W * k0:GRID_W * (used[-1] + 1), :], qm)
            for kl in used:
                kr = Q_ROWS * blk + kl
                for c in range(lane_tiles):
                    if plan[kr][c] == TILE_SKIP:
                        continue
                    rows, lanes = tile(kr, c)
                    s = d[GRID_W * (kl - k0):GRID_W * (kl - k0 + 1), lanes] + bias_ref[j, e, rows, lanes]
                    if plan[kr][c] == TILE_MASKED:
                        s = s + rm_ref[0, kr:kr + 1, lanes]
                    s_sc[rows, lanes] = s
                    mx[c] = s if mx[c] is None else jnp.maximum(mx[c], s)
        m_sc[...] = jnp.concatenate(mx, axis=1)

    def finish(j, e, s_sc, m_sc, plan):
        lo, hi = HEAD_DIM * e, HEAD_DIM * (e + 1)
        m = jnp.max(m_sc[...], axis=0, keepdims=True)
        o = None
        for blk, v_ref in enumerate((vp_ref, vc_ref, vn_ref)):
            p_rows = []
            for kl in range(Q_ROWS):
                kr = Q_ROWS * blk + kl
                p_tiles = []
                for c in range(lane_tiles):
                    rows, lanes = tile(kr, c)
                    p_tiles.append(zero_tile if plan[kr][c] == TILE_SKIP else
                                   jnp.exp2(s_sc[rows, lanes] - m[:, lanes]).astype(BF16))
                p_rows.append(jnp.concatenate(p_tiles, axis=1))
            pb = jnp.concatenate(p_rows, axis=0)
            v = jnp.concatenate([v_ref[0, 0, j, lo:hi, :], ones], axis=0)
            ob = _dot(v, pb)
            o = ob if o is None else o + ob
        o_ref[0, 0, j, lo:hi, :] = (o[:HEAD_DIM] / o[HEAD_DIM:HEAD_DIM + 1]).astype(BF16)

    def all_heads(plan):
        bufs = ((sa_sc, ma_sc), (sb_sc, mb_sc))

        def stage(pair0, t):
            scores(pair0 + (t + 1) // 2, (t + 1) % 2, *bufs[(t + 1) % 2], plan)
            finish(pair0 + t // 2, t % 2, *bufs[t % 2], plan)

        scores(0, 0, *bufs[0], plan)

        def body(i, carry):
            for t in range(HEADS_PER_TRIP):
                stage((HEADS_PER_TRIP // 2) * i, t)
            return carry

        trips = (N_HEADS - 1) // HEADS_PER_TRIP
        lax.fori_loop(0, trips, body, 0)
        for t in range(trips * HEADS_PER_TRIP, N_HEADS - 1):
            stage(0, t)
        finish(HEAD_PAIRS - 1, 1, *bufs[1], plan)

    interior = jnp.logical_and(pl.program_id(1) > 0, pl.program_id(1) < pl.num_programs(1) - 1)
    pl.when(interior)(lambda: all_heads(_INTERIOR_PLAN))
    pl.when(jnp.logical_not(interior))(lambda: all_heads(_EDGE_PLAN))


def _attn_call(q_t, k, v_t, hm, bias, rmask):
    b, nblk = q_t.shape[:2]
    prev = lambda j: jnp.maximum(j - 1, 0)
    nxt = lambda j: jnp.minimum(j + 1, nblk - 1)
    variant = lambda j: (j > 0).astype(jnp.int32) + (j == nblk - 1).astype(jnp.int32)
    kspec = lambda f: pl.BlockSpec((1, HEAD_PAIRS, QB, PAIR_W), lambda i, j: (i, 0, f(j), 0))
    vspec = lambda f: pl.BlockSpec((1, 1, HEAD_PAIRS, PAIR_W, QB), lambda i, j: (i, f(j), 0, 0, 0))
    same = lambda j: j
    return pl.pallas_call(
        _attn_kernel,
        grid=(b, nblk),
        in_specs=[
            vspec(same),
            kspec(prev), kspec(same), kspec(nxt),
            vspec(prev), vspec(same), vspec(nxt),
            pl.BlockSpec((2, PAIR_W, QB), lambda i, j: (0, 0, 0)),
            pl.BlockSpec((HEAD_PAIRS, 2, KB, QB), lambda i, j: (0, 0, 0, 0), pipeline_mode=pl.Buffered(1)),
            pl.BlockSpec((1, RM_ROWS, QB), lambda i, j: (variant(j), 0, 0)),
        ],
        out_specs=vspec(same),
        out_shape=jax.ShapeDtypeStruct(q_t.shape, BF16),
        scratch_shapes=[pltpu.VMEM((KB, QB), F32)] * 2 + [pltpu.VMEM((GRID_W, QB), F32)] * 2,
        compiler_params=_cparams(("parallel", "parallel")),
        name="nbr_attention",
    )(q_t, k, k, k, v_t, v_t, v_t, hm, bias, rmask)


def _attn_tables(rpb):
    kr = np.arange(3 * Q_ROWS)[:, None, None, None]
    kc = np.arange(GRID_W)[None, :, None, None]
    r = np.arange(Q_ROWS)[None, None, :, None]
    c = np.arange(GRID_W)[None, None, None, :]
    shape = (3 * Q_ROWS, GRID_W, Q_ROWS, GRID_W)
    cs = np.clip(c - WIN_COLS // 2, 0, GRID_W - WIN_COLS)
    col_ok = ((kc >= cs) & (kc < cs + WIN_COLS))[0, :, 0, :]
    dc = (kc - c + WIN_COLS - 1)[0, :, 0, :]
    onehot = ((dc[None] == np.arange(2 * WIN_COLS - 1)[:, None, None]) & col_ok[None]).astype(np.float32)
    toep = jnp.einsum('hrd,dkc->hrkc', rpb.astype(F32), onehot, precision=lax.Precision.HIGHEST)
    toep = jnp.where(col_ok[None, None], toep * LOG2E, NEG)
    off = WIN_ROWS - 1 - Q_ROWS
    bias = jnp.concatenate([toep[:, off - rr:off - rr + 3 * Q_ROWS] for rr in range(Q_ROWS)], axis=-1)
    bias = bias.reshape(HEAD_PAIRS, 2, KB, QB)
    rshape = (3 * Q_ROWS, 1, Q_ROWS, GRID_W)
    first = np.broadcast_to(kr >= Q_ROWS, rshape)
    inner = np.broadcast_to((kr >= r) & (kr < r + WIN_ROWS), rshape)
    last = np.broadcast_to(kr < 2 * Q_ROWS, rshape)
    rmask = np.where(np.stack([first, inner, last]), 0.0, NEG).astype(np.float32).reshape(3, 3 * Q_ROWS, QB)
    rmask = np.pad(rmask, ((0, 0), (0, RM_ROWS - 3 * Q_ROWS), (0, 0)))
    return bias, jnp.asarray(rmask)


def _ffn(x0, gf_ref, wgu_ref, wd_ref):
    hb = _rms(x0, gf_ref[...]).astype(BF16)
    gu = _dot(hb, wgu_ref[...])
    g, u = gu[:, :D_FF], gu[:, D_FF:]
    act = (g * (1.0 / (1.0 + jnp.exp(-g))) * u).astype(BF16)
    return x0 + _dot(act, wd_ref[...])


def _resident(shape):
    return pl.BlockSpec(shape, lambda *idx: (0,) * len(shape), pipeline_mode=pl.Buffered(1))


def _phase_major(x):
    return jnp.swapaxes(x.reshape(-1, CHUNK, D_MODEL), 0, 1)


def _store_chunk_major(ref, x):
    xs = _phase_major(x)
    for s in range(CHUNK):
        ref[:, D_MODEL * s:D_MODEL * (s + 1)] = xs[s]


def _attn_ffn_kernel(x_ref, o_ref, wo_ref, gf_ref, wgu_ref, wd_ref, gn_ref, x1_ref, h1_ref):
    parts = []
    for r in range(ROW_SPLIT):
        o = o_ref[0, r].reshape(D_MODEL, QB)
        x0 = x_ref[0, QB * r:QB * (r + 1), :] + _dot_tn(o, wo_ref[...])
        parts.append(_ffn(x0, gf_ref, wgu_ref, wd_ref))
    x1 = jnp.concatenate(parts, axis=0)
    _store_chunk_major(x1_ref, x1)
    _store_chunk_major(h1_ref, _rms(x1, gn_ref[...]))


def _attn_ffn_call(x, o_t, wo, gf, wgu, wd, gn, tm):
    b, l, _ = x.shape
    nt = l // tm
    out = pl.BlockSpec((tm // CHUNK, CHUNK * D_MODEL), lambda i, j: (i * nt + j, 0))
    return pl.pallas_call(
        _attn_ffn_kernel,
        grid=(b, nt),
        in_specs=[
            pl.BlockSpec((1, tm, D_MODEL), lambda i, j: (i, j, 0)),
            pl.BlockSpec((1, ROW_SPLIT, HEAD_PAIRS, PAIR_W, QB), lambda i, j: (i, j, 0, 0, 0)),
            _resident((D_MODEL, D_MODEL)),
            _resident((1, D_MODEL)),
            _resident((D_MODEL, 2 * D_FF)),
            _resident((D_FF, D_MODEL)),
            _resident((1, D_MODEL)),
        ],
        out_specs=[out, out],
        out_shape=[jax.ShapeDtypeStruct((b * l // CHUNK, CHUNK * D_MODEL), F32)] * 2,
        compiler_params=_cparams(("parallel", "parallel")),
        name="attn_out_ffn",
    )(x, o_t, wo, gf, wgu, wd, gn)


def _glu_ffn_kernel(x_ref, z_ref, wglu_ref, gf_ref, wgu_ref, wd_ref, y_ref):
    parts = []
    for r in range(ROW_SPLIT):
        ph = range(r * CHUNK // ROW_SPLIT, (r + 1) * CHUNK // ROW_SPLIT)
        z = z_ref[ph.start:ph.stop].reshape(-1, D_MODEL)
        x = jnp.concatenate([x_ref[:, D_MODEL * s:D_MODEL * (s + 1)] for s in ph], axis=0)
        zz = _dot(z, wglu_ref[...])
        za, zg = zz[:, :D_MODEL], zz[:, D_MODEL:]
        x0 = x + za * (1.0 / (1.0 + jnp.exp(-zg)))
        parts.append(_ffn(x0, gf_ref, wgu_ref, wd_ref))
    y = jnp.concatenate(parts, axis=0).reshape(CHUNK, -1, D_MODEL)
    y_ref[...] = jnp.swapaxes(y, 0, 1).reshape(-1, D_MODEL)


def _glu_ffn_call(x_ph, z_pm, wglu, gf, wgu, wd, tm):
    nc = x_ph.shape[0]
    nk = tm // CHUNK
    return pl.pallas_call(
        _glu_ffn_kernel,
        grid=(nc // nk,),
        in_specs=[
            pl.BlockSpec((nk, CHUNK * D_MODEL), lambda i: (i, 0)),
            pl.BlockSpec((CHUNK, nk, D_MODEL), lambda i: (0, i, 0)),
            _resident((D_MODEL, 2 * D_MODEL)),
            _resident((1, D_MODEL)),
            _resident((D_MODEL, 2 * D_FF)),
            _resident((D_FF, D_MODEL)),
        ],
        out_specs=pl.BlockSpec((tm, D_MODEL), lambda i: (i, 0)),
        out_shape=jax.ShapeDtypeStruct((nc * CHUNK, D_MODEL), F32),
        compiler_params=_cparams(("parallel",)),
        name="glu_out_ffn",
    )(x_ph, z_pm, wglu, gf, wgu, wd)


def _chunk_operands(h_refs, ut_sc, ht_sc):
    for s, h_ref in enumerate(h_refs):
        ht = h_ref[...].T
        if ht_sc is not None:
            ht_sc[s] = ht
        ut_sc[:, SSM_GROUP * s:SSM_GROUP * (s + 1), :] = (
            ht.reshape(GROUP_BLOCK, SSM_GROUP, -1).astype(BF16))


def _s5_state_kernel(*refs):
    h_refs = refs[:CHUNK]
    pt_ref, sf_ref, sb_ref, ut_sc, st_sc = refs[CHUNK:]
    _chunk_operands(h_refs, ut_sc, None)
    for g in range(GROUP_BLOCK):
        st_sc[g] = _dot(pt_ref[g], ut_sc[g]).T
    sf_ref[...] = jnp.swapaxes(st_sc[:, :, :SW], 0, 1)
    sb_ref[...] = jnp.swapaxes(st_sc[:, :, SW:], 0, 1)


def _h_specs(nk):
    return [pl.BlockSpec((nk, GB_W), lambda gb, i, s=s: (i, s * N_GB + gb)) for s in range(CHUNK)]


def _s5_state_call(h_ph, pt, nk):
    nc = h_ph.shape[0]
    sspec = pl.BlockSpec((nk, GROUP_BLOCK, SW), lambda gb, i: (i, gb, 0))
    return pl.pallas_call(
        _s5_state_kernel,
        grid=(N_GB, nc // nk),
        in_specs=[*_h_specs(nk),
                  pl.BlockSpec((GROUP_BLOCK, 2 * SW, CW), lambda gb, i: (gb, 0, 0))],
        out_specs=[sspec, sspec],
        out_shape=[jax.ShapeDtypeStruct((nc, N_GROUPS, SW), F32)] * 2,
        scratch_shapes=[pltpu.VMEM((GROUP_BLOCK, CW, nk), BF16),
                        pltpu.VMEM((GROUP_BLOCK, nk, 2 * SW), F32)],
        compiler_params=_cparams(("parallel", "parallel")),
        name="s5_chunk_states",
    )(*([h_ph] * CHUNK), pt)


def _scan_kernel(sf_ref, sb_ref, ar_ref, ai_ref, xf_ref, xb_ref, st_sc):
    kc = sf_ref.shape[1]

    @pl.when(pl.program_id(1) == 0)
    def _():
        st_sc[...] = jnp.zeros_like(st_sc)

    a = [(ar_ref[d], ai_ref[d]) for d in range(2)]

    def step(d, s_ref, x_ref, k, x, xs):
        ar, ai = a[d]
        s = s_ref[0, k]
        x_ref[0, k] = x.astype(BF16)
        return ar * x + ai * xs + s, ar * xs - ai * x + pltpu.roll(s, STATE, 1)

    def body(i, carry):
        xf, xfs, xb, xbs = carry
        return (*step(0, sf_ref, xf_ref, i, xf, xfs), *step(1, sb_ref, xb_ref, kc - 1 - i, xb, xbs))

    init = (st_sc[0, 0], st_sc[0, 1], st_sc[1, 0], st_sc[1, 1])
    xf, xfs, xb, xbs = lax.fori_loop(0, kc, body, init, unroll=8)
    st_sc[0, 0], st_sc[0, 1], st_sc[1, 0], st_sc[1, 1] = xf, xfs, xb, xbs


def _scan_call(sf, sb, ar, ai, kc):
    b, ncs = sf.shape[:2]
    nkb = ncs // kc
    fwd = pl.BlockSpec((1, kc, N_GROUPS, SW), lambda i, j: (i, j, 0, 0))
    bwd = pl.BlockSpec((1, kc, N_GROUPS, SW), lambda i, j: (i, nkb - 1 - j, 0, 0))
    tab = pl.BlockSpec((2, N_GROUPS, SW), lambda i, j: (0, 0, 0))
    return pl.pallas_call(
        _scan_kernel,
        grid=(b, nkb),
        in_specs=[fwd, bwd, tab, tab],
        out_specs=[fwd, bwd],
        out_shape=[jax.ShapeDtypeStruct(sf.shape, BF16)] * 2,
        scratch_shapes=[pltpu.VMEM((2, 2, N_GROUPS, SW), F32)],
        compiler_params=_cparams(("parallel", "arbitrary")),
        name="s5_chunk_scan",
    )(sf, sb, ar, ai)


def _s5_out_kernel(*refs):
    h_refs = refs[:CHUNK]
    xf_ref, xb_ref, mt_ref, qt_ref, d_ref, z_ref, ut_sc, ht_sc, yt_sc = refs[CHUNK:]
    _chunk_operands(h_refs, ut_sc, ht_sc)
    xf = jnp.swapaxes(xf_ref[...].astype(F32), 0, 1)
    xb = jnp.swapaxes(xb_ref[...].astype(F32), 0, 1)
    for g in range(GROUP_BLOCK):
        xin = jnp.concatenate([xf[g], xb[g]], axis=1).astype(BF16)
        yt_sc[g] = _dot(mt_ref[g], ut_sc[g]) + _dot_nt(qt_ref[g], xin)
    d = d_ref[...]
    for t in range(CHUNK):
        y = yt_sc[:, SSM_GROUP * t:SSM_GROUP * (t + 1), :].reshape(GB_W, -1)
        v = y + d * ht_sc[t]
        z_ref[t] = jax.nn.gelu(v, approximate=True).T.astype(BF16)


def _s5_out_call(h_ph, xin_f, xin_b, mt, qt, d_col, nk):
    nc = h_ph.shape[0]
    xspec = pl.BlockSpec((nk, GROUP_BLOCK, SW), lambda gb, i: (i, gb, 0))
    wspec = pl.BlockSpec((GROUP_BLOCK, CW, CW), lambda gb, i: (gb, 0, 0))
    return pl.pallas_call(
        _s5_out_kernel,
        grid=(N_GB, nc // nk),
        in_specs=[*_h_specs(nk), xspec, xspec, wspec, wspec,
                  pl.BlockSpec((GB_W, nk), lambda gb, i: (gb, 0))],
        out_specs=pl.BlockSpec((CHUNK, nk, GB_W), lambda gb, i: (0, i, gb)),
        out_shape=jax.ShapeDtypeStruct((CHUNK, nc, D_MODEL), BF16),
        scratch_shapes=[
            pltpu.VMEM((GROUP_BLOCK, CW, nk), BF16),
            pltpu.VMEM((CHUNK, GB_W, nk), F32),
            pltpu.VMEM((GROUP_BLOCK, CW, nk), F32),
        ],
        compiler_params=_cparams(("parallel", "parallel")),
        name="s5_outputs",
    )(*([h_ph] * CHUNK), xin_f, xin_b, mt, qt, d_col)


def _s5_matrices(lam_re, lam_im, log_step, b_re, b_im, c_re, c_im):
    hp = lax.Precision.HIGHEST
    dt = jnp.exp(log_step.astype(F32))[..., None]
    lam_re = lam_re.astype(F32)
    lam_im = lam_im.astype(F32)
    mag = jnp.exp(lam_re * dt)
    ang = lam_im * dt
    lb_re = mag * jnp.cos(ang)
    lb_im = mag * jnp.sin(ang)
    den = lam_re * lam_re + lam_im * lam_im
    nr = lb_re - 1.0
    ni = lb_im
    coef_re = (nr * lam_re + ni * lam_im) / den
    coef_im = (ni * lam_re - nr * lam_im) / den
    b_re = b_re.astype(F32)
    b_im = b_im.astype(F32)
    bb_re = coef_re[..., None] * b_re - coef_im[..., None] * b_im
    bb_im = coef_re[..., None] * b_im + coef_im[..., None] * b_re
    c_re = c_re.astype(F32)
    c_im = c_im.astype(F32)

    pw_re = [jnp.ones_like(lb_re)]
    pw_im = [jnp.zeros_like(lb_im)]
    for _ in range(CHUNK):
        pr, pi = pw_re[-1], pw_im[-1]
        pw_re.append(pr * lb_re - pi * lb_im)
        pw_im.append(pr * lb_im + pi * lb_re)
    pw_re = jnp.stack(pw_re, axis=-1)
    pw_im = jnp.stack(pw_im, axis=-1)

    bbl_re = jnp.tile(bb_re, (1, 1, 1, CHUNK))
    bbl_im = jnp.tile(bb_im, (1, 1, 1, CHUNK))

    def response(pr, pi):
        prl = jnp.repeat(pr, SSM_GROUP, axis=-1)
        pil = jnp.repeat(pi, SSM_GROUP, axis=-1)
        return prl * bbl_re - pil * bbl_im, prl * bbl_im + pil * bbl_re

    w_re, w_im = response(pw_re[..., :CHUNK], pw_im[..., :CHUNK])
    wr_re, wr_im = response(pw_re[..., CHUNK - 1::-1], pw_im[..., CHUNK - 1::-1])

    kmat = (jnp.einsum('dgcn,dgnx->dgcx', c_re, w_re, precision=hp)
            - jnp.einsum('dgcn,dgnx->dgcx', c_im, w_im, precision=hp))
    lane = jnp.arange(CW)
    step, chan = lane // SSM_GROUP, lane % SSM_GROUP
    same_c = chan[:, None] == chan[None, :]
    t_i = jnp.arange(CHUNK)[:, None, None]
    sel_f = (same_c[None] & (step[None, :, None] == t_i - step[None, None, :])).astype(F32)
    sel_b = (same_c[None] & (step[None, :, None] == step[None, None, :] - t_i)).astype(F32)
    mt = (jnp.einsum('gcx,txy->gtcy', kmat[0], sel_f, precision=hp)
          + jnp.einsum('gcx,txy->gtcy', kmat[1], sel_b, precision=hp)).reshape(N_GROUPS, CW, CW)

    pt = jnp.stack([wr_re[0], wr_im[0], w_re[1], w_im[1]], axis=1).reshape(N_GROUPS, 2 * SW, CW)

    def carry(d, pr, pi):
        cr = c_re[d][:, None]
        ci = c_im[d][:, None]
        pr = pr.transpose(0, 2, 1)[:, :, None, :]
        pi = pi.transpose(0, 2, 1)[:, :, None, :]
        return [cr * pr - ci * pi, -(cr * pi + ci * pr)]

    qt = jnp.concatenate(carry(0, pw_re[0, ..., 1:], pw_im[0, ..., 1:])
                         + carry(1, pw_re[1, ..., :0:-1], pw_im[1, ..., :0:-1]), axis=-1)
    qt = qt.reshape(N_GROUPS, CW, 2 * SW)

    a_re = pw_re[..., CHUNK]
    a_im = pw_im[..., CHUNK]
    ar = jnp.concatenate([a_re, a_re], axis=-1)
    ai = jnp.concatenate([-a_im, a_im], axis=-1)
    return mt.astype(BF16), pt.astype(BF16), qt.astype(BF16), ar, ai


def _trunk(x, p, tm, nk):
    b, l, _ = x.shape
    nc = b * l // CHUNK
    kc = min(128, l // CHUNK)
    q_t, k, v_t = _qkv_call(x, p['g_mix0'], p['wqkv_t'], p['gq'], p['gk'], tm)
    o_t = _attn_call(q_t, k, v_t, p['hm'], p['bias'], p['rmask'])
    x_ph, h_ph = _attn_ffn_call(x, o_t, p['wo'], p['g_ffn0'], p['wgu0'], p['wd0'], p['g_mix1'], tm)

    sf, sb = _s5_state_call(h_ph, p['pt'], nk)
    sshape = (b, l // CHUNK, N_GROUPS, SW)
    xin_f, xin_b = _scan_call(sf.reshape(sshape), sb.reshape(sshape), p['ar'], p['ai'], kc)
    z_pm = _s5_out_call(h_ph, xin_f.reshape(nc, N_GROUPS, SW), xin_b.reshape(nc, N_GROUPS, SW),
                        p['mt'], p['qt'], p['d_col'], nk)
    y = _glu_ffn_call(x_ph, z_pm, p['wglu'], p['g_ffn1'], p['wgu1'], p['wd1'], tm)
    return y.reshape(b, l, D_MODEL)


def kernel(x_prompt, x_sample, norm_mix, norm_ffn, w_qkv, w_o, q_gain, k_gain, rpb, lam_re, lam_im, log_step,
           b_re, b_im, c_re, c_im, d_skip, w_glu, w_gate_up, w_down):
    tm, nk = 512, 128
    scale = HEAD_DIM ** -0.5 * LOG2E
    mt, pt, qt, ar, ai = _s5_matrices(lam_re[0], lam_im[0], log_step[0], b_re[0], b_im[0], c_re[0], c_im[0])
    bias, rmask = _attn_tables(rpb[0])
    rows = np.arange(PAIR_W)[None, :, None]
    hm = np.broadcast_to((rows // HEAD_DIM) == np.arange(2)[:, None, None], (2, PAIR_W, QB))
    p = dict(
        g_mix0=norm_mix[0].astype(F32)[None], g_mix1=norm_mix[1].astype(F32)[None],
        g_ffn0=norm_ffn[0].astype(F32)[None], g_ffn1=norm_ffn[1].astype(F32)[None],
        wqkv_t=w_qkv[0].T.astype(BF16), wo=w_o[0].astype(BF16),
        gq=jnp.broadcast_to((q_gain[0].astype(F32) * scale)[:, None], (HEAD_DIM, tm // ROW_SPLIT)),
        gk=jnp.broadcast_to(k_gain[0].astype(F32)[:, None], (HEAD_DIM, tm // ROW_SPLIT)),
        hm=jnp.asarray(hm, BF16), bias=bias, rmask=rmask,
        wgu0=w_gate_up[0].astype(BF16), wd0=w_down[0].astype(BF16),
        wgu1=w_gate_up[1].astype(BF16), wd1=w_down[1].astype(BF16),
        wglu=w_glu[0].astype(BF16),
        mt=mt, pt=pt, qt=qt, ar=ar, ai=ai,
        d_col=jnp.broadcast_to(d_skip[0].astype(F32)[:, None], (D_MODEL, nk)),
    )
    return _trunk(x_prompt, p, tm, nk), _trunk(x_sample, p, tm, nk)
```

```python
import functools

import numpy as np
import jax
import jax.numpy as jnp
from jax import lax
from jax.experimental import pallas as pl
from jax.experimental.pallas import tpu as pltpu

D_MODEL = 1024
GRID_W = 64
N_HEADS = 16
HEAD_DIM = D_MODEL // N_HEADS
WIN_ROWS = 8
WIN_COLS = 16
SSM_GROUP = 16
N_GROUPS = D_MODEL // SSM_GROUP
STATE = 64
D_FF = 2816
EPS = 1e-6

F32 = jnp.float32
BF16 = jnp.bfloat16

V7X_LANES = 128
V7X_VMEM_BYTES = 64 * 1024 * 1024
VMEM_LIMIT = (V7X_VMEM_BYTES * 7) // 8

HEAD_PAIRS = N_HEADS // 2
PAIR_W = 2 * HEAD_DIM
Q_ROWS = 4
QB = Q_ROWS * GRID_W
KB = 3 * QB
NEG = -1e30
ONES_ROWS = 16
LOG2E = 1.4426950408889634
RM_ROWS = 16
TILE_SKIP, TILE_OPEN, TILE_MASKED = 0, 1, 2
HEADS_PER_TRIP = 8


def _tile_plans():
    assert 2 * Q_ROWS <= WIN_ROWS and V7X_LANES % GRID_W == 0
    per_tile = V7X_LANES // GRID_W
    interior, edge = [], []
    for kr in range(3 * Q_ROWS):
        irow, erow = [], []
        for c in range(QB // V7X_LANES):
            ok = [r <= kr < r + WIN_ROWS for r in range(per_tile * c, per_tile * (c + 1))]
            irow.append(TILE_OPEN if all(ok) else TILE_MASKED if any(ok) else TILE_SKIP)
            erow.append(TILE_OPEN if Q_ROWS <= kr < 2 * Q_ROWS else TILE_MASKED)
        interior.append(irow)
        edge.append(erow)
    return interior, edge


_INTERIOR_PLAN, _EDGE_PLAN = _tile_plans()

CHUNK = 16
CW = CHUNK * SSM_GROUP
SW = 2 * STATE
GROUP_BLOCK = 16
GB_W = GROUP_BLOCK * SSM_GROUP
N_GB = N_GROUPS // GROUP_BLOCK
ROW_SPLIT = 2


def _cparams(sem):
    return pltpu.CompilerParams(dimension_semantics=sem, vmem_limit_bytes=VMEM_LIMIT)


def _rms(x, g):
    return x * lax.rsqrt(jnp.mean(x * x, axis=-1, keepdims=True) + EPS) * g


def _dot(a, b):
    return jnp.dot(a, b, preferred_element_type=F32)


def _dot_nt(a, b):
    return lax.dot_general(a, b, (((1,), (1,)), ((), ())), preferred_element_type=F32)


def _dot_tn(a, b):
    return lax.dot_general(a, b, (((0,), (0,)), ((), ())), preferred_element_type=F32)


def _qkv_kernel(x_ref, g_ref, w_ref, gq_ref, gk_ref, q_ref, k_ref, v_ref):
    tm = x_ref.shape[1] // ROW_SPLIT

    def head_norm(t, gain):
        t3 = t.reshape(N_HEADS, HEAD_DIM, tm)
        y = t3 * lax.rsqrt(jnp.mean(t3 * t3, axis=1, keepdims=True) + EPS)
        return y * gain[None]

    for r in range(ROW_SPLIT):
        tok = slice(r * tm, (r + 1) * tm)
        h = _rms(x_ref[0, tok, :], g_ref[...]).astype(BF16)
        q = head_norm(_dot_nt(w_ref[0:D_MODEL, :], h), gq_ref[...])
        k = head_norm(_dot_nt(w_ref[D_MODEL:2 * D_MODEL, :], h), gk_ref[...])
        v = _dot_nt(w_ref[2 * D_MODEL:3 * D_MODEL, :], h)
        q_ref[0, r] = q.reshape(HEAD_PAIRS, PAIR_W, tm).astype(BF16)
        v_ref[0, r] = v.reshape(HEAD_PAIRS, PAIR_W, tm).astype(BF16)
        kt = k.reshape(D_MODEL, tm).T
        for j in range(HEAD_PAIRS):
            k_ref[0, j, tok, :] = kt[:, PAIR_W * j:PAIR_W * (j + 1)].astype(BF16)


def _qkv_call(x, g, w_t, gq, gk, tm):
    b, l, _ = x.shape
    assert tm // ROW_SPLIT == QB
    blocked = pl.BlockSpec((1, ROW_SPLIT, HEAD_PAIRS, PAIR_W, QB), lambda i, j: (i, j, 0, 0, 0))
    return pl.pallas_call(
        _qkv_kernel,
        grid=(b, l // tm),
        in_specs=[
            pl.BlockSpec((1, tm, D_MODEL), lambda i, j: (i, j, 0)),
            pl.BlockSpec((1, D_MODEL), lambda i, j: (0, 0)),
            pl.BlockSpec((3 * D_MODEL, D_MODEL), lambda i, j: (0, 0)),
            pl.BlockSpec((HEAD_DIM, tm // ROW_SPLIT), lambda i, j: (0, 0)),
            pl.BlockSpec((HEAD_DIM, tm // ROW_SPLIT), lambda i, j: (0, 0)),
        ],
        out_specs=[
            blocked,
            pl.BlockSpec((1, HEAD_PAIRS, tm, PAIR_W), lambda i, j: (i, 0, j, 0)),
            blocked,
        ],
        out_shape=[
            jax.ShapeDtypeStruct((b, l // QB, HEAD_PAIRS, PAIR_W, QB), BF16),
            jax.ShapeDtypeStruct((b, HEAD_PAIRS, l, PAIR_W), BF16),
            jax.ShapeDtypeStruct((b, l // QB, HEAD_PAIRS, PAIR_W, QB), BF16),
        ],
        compiler_params=_cparams(("parallel", "parallel")),
        name="qkv_proj",
    )(x, g, w_t, gq, gk)


def _attn_kernel(q_ref, kp_ref, kc_ref, kn_ref, vp_ref, vc_ref, vn_ref, hm_ref, bias_ref, rm_ref, o_ref,
                 sa_sc, sb_sc, ma_sc, mb_sc):
    ones = jnp.ones((ONES_ROWS, QB), BF16)
    zero_tile = jnp.zeros((GRID_W, V7X_LANES), BF16)
    lane_tiles = QB // V7X_LANES

    def tile(kr, c):
        return slice(GRID_W * kr, GRID_W * (kr + 1)), slice(V7X_LANES * c, V7X_LANES * (c + 1))

    def scores(j, e, s_sc, m_sc, plan):
        qm = q_ref[0, 0, j] * hm_ref[e]
        mx = [None] * lane_tiles
        for blk, k_ref in enumerate((kp_ref, kc_ref, kn_ref)):
            used = [kl for kl in range(Q_ROWS) if any(t != TILE_SKIP for t in plan[Q_ROWS * blk + kl])]
            k0 = used[0]
            d = _dot(k_ref[0, j, GRID_W * k0:GRID_W * (used[-1] + 1), :], qm)
            for kl in used:
                kr = Q_ROWS * blk + kl
                for c in range(lane_tiles):
                    if plan[kr][c] == TILE_SKIP:
                        continue
                    rows, lanes = tile(kr, c)
                    s = d[GRID_W * (kl - k0):GRID_W * (kl - k0 + 1), lanes] + bias_ref[j, e, rows, lanes]
                    if plan[kr][c] == TILE_MASKED:
                        s = s + rm_ref[0, kr:kr + 1, lanes]
                    s_sc[rows, lanes] = s
                    mx[c] = s if mx[c] is None else jnp.maximum(mx[c], s)
        m_sc[...] = jnp.concatenate(mx, axis=1)

    def finish(j, e, s_sc, m_sc, plan):
        lo, hi = HEAD_DIM * e, HEAD_DIM * (e + 1)
        m = jnp.max(m_sc[...], axis=0, keepdims=True)
        o = None
        for blk, v_ref in enumerate((vp_ref, vc_ref, vn_ref)):
            p_rows = []
            for kl in range(Q_ROWS):
                kr = Q_ROWS * blk + kl
                p_tiles = []
                for c in range(lane_tiles):
                    rows, lanes = tile(kr, c)
                    p_tiles.append(zero_tile if plan[kr][c] == TILE_SKIP else
                                   jnp.exp2(s_sc[rows, lanes] - m[:, lanes]).astype(BF16))
                p_rows.append(jnp.concatenate(p_tiles, axis=1))
            pb = jnp.concatenate(p_rows, axis=0)
            v = jnp.concatenate([v_ref[0, 0, j, lo:hi, :], ones], axis=0)
            ob = _dot(v, pb)
            o = ob if o is None else o + ob
        o_ref[0, 0, j, lo:hi, :] = (o[:HEAD_DIM] / o[HEAD_DIM:HEAD_DIM + 1]).astype(BF16)

    def all_heads(plan):
        bufs = ((sa_sc, ma_sc), (sb_sc, mb_sc))

        def stage(pair0, t):
            scores(pair0 + (t + 1) // 2, (t + 1) % 2, *bufs[(t + 1) % 2], plan)
            finish(pair0 + t // 2, t % 2, *bufs[t % 2], plan)

        scores(0, 0, *bufs[0], plan)

        def body(i, carry):
            for t in range(HEADS_PER_TRIP):
                stage((HEADS_PER_TRIP // 2) * i, t)
            return carry

        trips = (N_HEADS - 1) // HEADS_PER_TRIP
        lax.fori_loop(0, trips, body, 0)
        for t in range(trips * HEADS_PER_TRIP, N_HEADS - 1):
            stage(0, t)
        finish(HEAD_PAIRS - 1, 1, *bufs[1], plan)

    interior = jnp.logical_and(pl.program_id(1) > 0, pl.program_id(1) < pl.num_programs(1) - 1)
    pl.when(interior)(lambda: all_heads(_INTERIOR_PLAN))
    pl.when(jnp.logical_not(interior))(lambda: all_heads(_EDGE_PLAN))


def _attn_call(q_t, k, v_t, hm, bias, rmask):
    b, nblk = q_t.shape[:2]
    prev = lambda j: jnp.maximum(j - 1, 0)
    nxt = lambda j: jnp.minimum(j + 1, nblk - 1)
    variant = lambda j: (j > 0).astype(jnp.int32) + (j == nblk - 1).astype(jnp.int32)
    kspec = lambda f: pl.BlockSpec((1, HEAD_PAIRS, QB, PAIR_W), lambda i, j: (i, 0, f(j), 0))
    vspec = lambda f: pl.BlockSpec((1, 1, HEAD_PAIRS, PAIR_W, QB), lambda i, j: (i, f(j), 0, 0, 0))
    same = lambda j: j
    return pl.pallas_call(
        _attn_kernel,
        grid=(b, nblk),
        in_specs=[
            vspec(same),
            kspec(prev), kspec(same), kspec(nxt),
            vspec(prev), vspec(same), vspec(nxt),
            pl.BlockSpec((2, PAIR_W, QB), lambda i, j: (0, 0, 0)),
            pl.BlockSpec((HEAD_PAIRS, 2, KB, QB), lambda i, j: (0, 0, 0, 0), pipeline_mode=pl.Buffered(1)),
            pl.BlockSpec((1, RM_ROWS, QB), lambda i, j: (variant(j), 0, 0)),
        ],
        out_specs=vspec(same),
        out_shape=jax.ShapeDtypeStruct(q_t.shape, BF16),
        scratch_shapes=[pltpu.VMEM((KB, QB), F32)] * 2 + [pltpu.VMEM((GRID_W, QB), F32)] * 2,
        compiler_params=_cparams(("parallel", "parallel")),
        name="nbr_attention",
    )(q_t, k, k, k, v_t, v_t, v_t, hm, bias, rmask)


def _attn_tables(rpb):
    kr = np.arange(3 * Q_ROWS)[:, None, None, None]
    kc = np.arange(GRID_W)[None, :, None, None]
    r = np.arange(Q_ROWS)[None, None, :, None]
    c = np.arange(GRID_W)[None, None, None, :]
    shape = (3 * Q_ROWS, GRID_W, Q_ROWS, GRID_W)
    cs = np.clip(c - WIN_COLS // 2, 0, GRID_W - WIN_COLS)
    col_ok = ((kc >= cs) & (kc < cs + WIN_COLS))[0, :, 0, :]
    dc = (kc - c + WIN_COLS - 1)[0, :, 0, :]
    onehot = ((dc[None] == np.arange(2 * WIN_COLS - 1)[:, None, None]) & col_ok[None]).astype(np.float32)
    toep = jnp.einsum('hrd,dkc->hrkc', rpb.astype(F32), onehot, precision=lax.Precision.HIGHEST)
    toep = jnp.where(col_ok[None, None], toep * LOG2E, NEG)
    off = WIN_ROWS - 1 - Q_ROWS
    bias = jnp.concatenate([toep[:, off - rr:off - rr + 3 * Q_ROWS] for rr in range(Q_ROWS)], axis=-1)
    bias = bias.reshape(HEAD_PAIRS, 2, KB, QB)
    rshape = (3 * Q_ROWS, 1, Q_ROWS, GRID_W)
    first = np.broadcast_to(kr >= Q_ROWS, rshape)
    inner = np.broadcast_to((kr >= r) & (kr < r + WIN_ROWS), rshape)
    last = np.broadcast_to(kr < 2 * Q_ROWS, rshape)
    rmask = np.where(np.stack([first, inner, last]), 0.0, NEG).astype(np.float32).reshape(3, 3 * Q_ROWS, QB)
    rmask = np.pad(rmask, ((0, 0), (0, RM_ROWS - 3 * Q_ROWS), (0, 0)))
    return bias, jnp.asarray(rmask)


def _ffn(x0, gf_ref, wgu_ref, wd_ref):
    hb = _rms(x0, gf_ref[...]).astype(BF16)
    g = _dot(hb, wgu_ref[:, :D_FF])
    u = _dot(hb, wgu_ref[:, D_FF:])
    act = (g * (1.0 / (1.0 + jnp.exp(-g))) * u).astype(BF16)
    return x0 + _dot(act, wd_ref[...])


def _resident(shape):
    return pl.BlockSpec(shape, lambda *idx: (0,) * len(shape), pipeline_mode=pl.Buffered(1))


def _phase_major(x):
    return jnp.swapaxes(x.reshape(-1, CHUNK, D_MODEL), 0, 1)


def _store_chunk_major(ref, x):
    xs = _phase_major(x)
    for s in range(CHUNK):
        ref[:, D_MODEL * s:D_MODEL * (s + 1)] = xs[s]


def _attn_ffn_kernel(x_ref, o_ref, wo_ref, gf_ref, wgu_ref, wd_ref, gn_ref, x1_ref, h1_ref):
    parts = []
    for r in range(ROW_SPLIT):
        o = o_ref[0, r].reshape(D_MODEL, QB)
        x0 = x_ref[0, QB * r:QB * (r + 1), :] + _dot_tn(o, wo_ref[...])
        parts.append(_ffn(x0, gf_ref, wgu_ref, wd_ref))
    x1 = jnp.concatenate(parts, axis=0)
    _store_chunk_major(x1_ref, x1)
    _store_chunk_major(h1_ref, _rms(x1, gn_ref[...]))


def _attn_ffn_call(x, o_t, wo, gf, wgu, wd, gn, tm):
    b, l, _ = x.shape
    nt = l // tm
    out = pl.BlockSpec((tm // CHUNK, CHUNK * D_MODEL), lambda i, j: (i * nt + j, 0))
    return pl.pallas_call(
        _attn_ffn_kernel,
        grid=(b, nt),
        in_specs=[
            pl.BlockSpec((1, tm, D_MODEL), lambda i, j: (i, j, 0)),
            pl.BlockSpec((1, ROW_SPLIT, HEAD_PAIRS, PAIR_W, QB), lambda i, j: (i, j, 0, 0, 0)),
            _resident((D_MODEL, D_MODEL)),
            _resident((1, D_MODEL)),
            _resident((D_MODEL, 2 * D_FF)),
            _resident((D_FF, D_MODEL)),
            _resident((1, D_MODEL)),
        ],
        out_specs=[out, out],
        out_shape=[jax.ShapeDtypeStruct((b * l // CHUNK, CHUNK * D_MODEL), F32)] * 2,
        compiler_params=_cparams(("parallel", "parallel")),
        name="attn_out_ffn",
    )(x, o_t, wo, gf, wgu, wd, gn)


def _glu_ffn_kernel(x_ref, z_ref, wglu_ref, gf_ref, wgu_ref, wd_ref, y_ref):
    parts = []
    for r in range(ROW_SPLIT):
        ph = range(r * CHUNK // ROW_SPLIT, (r + 1) * CHUNK // ROW_SPLIT)
        z = z_ref[ph.start:ph.stop].reshape(-1, D_MODEL)
        x = jnp.concatenate([x_ref[:, D_MODEL * s:D_MODEL * (s + 1)] for s in ph], axis=0)
        zz = _dot(z, wglu_ref[...])
        za, zg = zz[:, :D_MODEL], zz[:, D_MODEL:]
        x0 = x + za * (1.0 / (1.0 + jnp.exp(-zg)))
        parts.append(_ffn(x0, gf_ref, wgu_ref, wd_ref))
    y = jnp.concatenate(parts, axis=0).reshape(CHUNK, -1, D_MODEL)
    y_ref[...] = jnp.swapaxes(y, 0, 1).reshape(-1, D_MODEL)


def _glu_ffn_call(x_ph, z_pm, wglu, gf, wgu, wd, tm):
    nc = x_ph.shape[0]
    nk = tm // CHUNK
    return pl.pallas_call(
        _glu_ffn_kernel,
        grid=(nc // nk,),
        in_specs=[
            pl.BlockSpec((nk, CHUNK * D_MODEL), lambda i: (i, 0)),
            pl.BlockSpec((CHUNK, nk, D_MODEL), lambda i: (0, i, 0)),
            _resident((D_MODEL, 2 * D_MODEL)),
            _resident((1, D_MODEL)),
            _resident((D_MODEL, 2 * D_FF)),
            _resident((D_FF, D_MODEL)),
        ],
        out_specs=pl.BlockSpec((tm, D_MODEL), lambda i: (i, 0)),
        out_shape=jax.ShapeDtypeStruct((nc * CHUNK, D_MODEL), F32),
        compiler_params=_cparams(("parallel",)),
        name="glu_out_ffn",
    )(x_ph, z_pm, wglu, gf, wgu, wd)


def _chunk_operands(h_refs, ut_sc, ht_sc):
    for s, h_ref in enumerate(h_refs):
        ht = h_ref[...].T
        if ht_sc is not None:
            ht_sc[s] = ht
        ut_sc[:, SSM_GROUP * s:SSM_GROUP * (s + 1), :] = (
            ht.reshape(GROUP_BLOCK, SSM_GROUP, -1).astype(BF16))


def _s5_state_kernel(*refs):
    h_refs = refs[:CHUNK]
    pt_ref, sf_ref, sb_ref, ut_sc, st_sc = refs[CHUNK:]
    _chunk_operands(h_refs, ut_sc, None)
    for g in range(GROUP_BLOCK):
        st_sc[g] = _dot(pt_ref[g], ut_sc[g]).T
    sf_ref[...] = jnp.swapaxes(st_sc[:, :, :SW], 0, 1)
    sb_ref[...] = jnp.swapaxes(st_sc[:, :, SW:], 0, 1)


def _h_specs(nk):
    return [pl.BlockSpec((nk, GB_W), lambda gb, i, s=s: (i, s * N_GB + gb)) for s in range(CHUNK)]


def _s5_state_call(h_ph, pt, nk):
    nc = h_ph.shape[0]
    sspec = pl.BlockSpec((nk, GROUP_BLOCK, SW), lambda gb, i: (i, gb, 0))
    return pl.pallas_call(
        _s5_state_kernel,
        grid=(N_GB, nc // nk),
        in_specs=[*_h_specs(nk),
                  pl.BlockSpec((GROUP_BLOCK, 2 * SW, CW), lambda gb, i: (gb, 0, 0))],
        out_specs=[sspec, sspec],
        out_shape=[jax.ShapeDtypeStruct((nc, N_GROUPS, SW), F32)] * 2,
        scratch_shapes=[pltpu.VMEM((GROUP_BLOCK, CW, nk), BF16),
                        pltpu.VMEM((GROUP_BLOCK, nk, 2 * SW), F32)],
        compiler_params=_cparams(("parallel", "parallel")),
        name="s5_chunk_states",
    )(*([h_ph] * CHUNK), pt)


def _scan_kernel(sf_ref, sb_ref, ar_ref, ai_ref, xf_ref, xb_ref, st_sc):
    kc = sf_ref.shape[1]

    @pl.when(pl.program_id(1) == 0)
    def _():
        st_sc[...] = jnp.zeros_like(st_sc)

    a = [(ar_ref[d], ai_ref[d]) for d in range(2)]

    def step(d, s_ref, x_ref, k, x, xs):
        ar, ai = a[d]
        s = s_ref[0, k]
        x_ref[0, k] = x.astype(BF16)
        return ar * x + ai * xs + s, ar * xs - ai * x + pltpu.roll(s, STATE, 1)

    def body(i, carry):
        xf, xfs, xb, xbs = carry
        return (*step(0, sf_ref, xf_ref, i, xf, xfs), *step(1, sb_ref, xb_ref, kc - 1 - i, xb, xbs))

    init = (st_sc[0, 0], st_sc[0, 1], st_sc[1, 0], st_sc[1, 1])
    xf, xfs, xb, xbs = lax.fori_loop(0, kc, body, init, unroll=8)
    st_sc[0, 0], st_sc[0, 1], st_sc[1, 0], st_sc[1, 1] = xf, xfs, xb, xbs


def _scan_call(sf, sb, ar, ai, kc):
    b, ncs = sf.shape[:2]
    nkb = ncs // kc
    fwd = pl.BlockSpec((1, kc, N_GROUPS, SW), lambda i, j: (i, j, 0, 0))
    bwd = pl.BlockSpec((1, kc, N_GROUPS, SW), lambda i, j: (i, nkb - 1 - j, 0, 0))
    tab = pl.BlockSpec((2, N_GROUPS, SW), lambda i, j: (0, 0, 0))
    return pl.pallas_call(
        _scan_kernel,
        grid=(b, nkb),
        in_specs=[fwd, bwd, tab, tab],
        out_specs=[fwd, bwd],
        out_shape=[jax.ShapeDtypeStruct(sf.shape, BF16)] * 2,
        scratch_shapes=[pltpu.VMEM((2, 2, N_GROUPS, SW), F32)],
        compiler_params=_cparams(("parallel", "arbitrary")),
        name="s5_chunk_scan",
    )(sf, sb, ar, ai)


def _s5_out_kernel(*refs):
    h_refs = refs[:CHUNK]
    xf_ref, xb_ref, mt_ref, qt_ref, d_ref, z_ref, ut_sc, ht_sc, yt_sc = refs[CHUNK:]
    _chunk_operands(h_refs, ut_sc, ht_sc)
    xf = jnp.swapaxes(xf_ref[...].astype(F32), 0, 1)
    xb = jnp.swapaxes(xb_ref[...].astype(F32), 0, 1)
    for g in range(GROUP_BLOCK):
        xin = jnp.concatenate([xf[g], xb[g]], axis=1).astype(BF16)
        yt_sc[g] = _dot(mt_ref[g], ut_sc[g]) + _dot_nt(qt_ref[g], xin)
    d = d_ref[...]
    for t in range(CHUNK):
        y = yt_sc[:, SSM_GROUP * t:SSM_GROUP * (t + 1), :].reshape(GB_W, -1)
        v = y + d * ht_sc[t]
        z_ref[t] = jax.nn.gelu(v, approximate=True).T.astype(BF16)


def _s5_out_call(h_ph, xin_f, xin_b, mt, qt, d_col, nk):
    nc = h_ph.shape[0]
    xspec = pl.BlockSpec((nk, GROUP_BLOCK, SW), lambda gb, i: (i, gb, 0))
    wspec = pl.BlockSpec((GROUP_BLOCK, CW, CW), lambda gb, i: (gb, 0, 0))
    return pl.pallas_call(
        _s5_out_kernel,
        grid=(N_GB, nc // nk),
        in_specs=[*_h_specs(nk), xspec, xspec, wspec, wspec,
                  pl.BlockSpec((GB_W, nk), lambda gb, i: (gb, 0))],
        out_specs=pl.BlockSpec((CHUNK, nk, GB_W), lambda gb, i: (0, i, gb)),
        out_shape=jax.ShapeDtypeStruct((CHUNK, nc, D_MODEL), BF16),
        scratch_shapes=[
            pltpu.VMEM((GROUP_BLOCK, CW, nk), BF16),
            pltpu.VMEM((CHUNK, GB_W, nk), F32),
            pltpu.VMEM((GROUP_BLOCK, CW, nk), F32),
        ],
        compiler_params=_cparams(("parallel", "parallel")),
        name="s5_outputs",
    )(*([h_ph] * CHUNK), xin_f, xin_b, mt, qt, d_col)


def _s5_matrices(lam_re, lam_im, log_step, b_re, b_im, c_re, c_im):
    hp = lax.Precision.HIGHEST
    dt = jnp.exp(log_step.astype(F32))[..., None]
    lam_re = lam_re.astype(F32)
    lam_im = lam_im.astype(F32)
    mag = jnp.exp(lam_re * dt)
    ang = lam_im * dt
    lb_re = mag * jnp.cos(ang)
    lb_im = mag * jnp.sin(ang)
    den = lam_re * lam_re + lam_im * lam_im
    nr = lb_re - 1.0
    ni = lb_im
    coef_re = (nr * lam_re + ni * lam_im) / den
    coef_im = (ni * lam_re - nr * lam_im) / den
    b_re = b_re.astype(F32)
    b_im = b_im.astype(F32)
    bb_re = coef_re[..., None] * b_re - coef_im[..., None] * b_im
    bb_im = coef_re[..., None] * b_im + coef_im[..., None] * b_re
    c_re = c_re.astype(F32)
    c_im = c_im.astype(F32)

    pw_re = [jnp.ones_like(lb_re)]
    pw_im = [jnp.zeros_like(lb_im)]
    for _ in range(CHUNK):
        pr, pi = pw_re[-1], pw_im[-1]
        pw_re.append(pr * lb_re - pi * lb_im)
        pw_im.append(pr * lb_im + pi * lb_re)
    pw_re = jnp.stack(pw_re, axis=-1)
    pw_im = jnp.stack(pw_im, axis=-1)

    bbl_re = jnp.tile(bb_re, (1, 1, 1, CHUNK))
    bbl_im = jnp.tile(bb_im, (1, 1, 1, CHUNK))

    def response(pr, pi):
        prl = jnp.repeat(pr, SSM_GROUP, axis=-1)
        pil = jnp.repeat(pi, SSM_GROUP, axis=-1)
        return prl * bbl_re - pil * bbl_im, prl * bbl_im + pil * bbl_re

    w_re, w_im = response(pw_re[..., :CHUNK], pw_im[..., :CHUNK])
    wr_re, wr_im = response(pw_re[..., CHUNK - 1::-1], pw_im[..., CHUNK - 1::-1])

    kmat = (jnp.einsum('dgcn,dgnx->dgcx', c_re, w_re, precision=hp)
            - jnp.einsum('dgcn,dgnx->dgcx', c_im, w_im, precision=hp))
    lane = jnp.arange(CW)
    step, chan = lane // SSM_GROUP, lane % SSM_GROUP
    same_c = chan[:, None] == chan[None, :]
    t_i = jnp.arange(CHUNK)[:, None, None]
    sel_f = (same_c[None] & (step[None, :, None] == t_i - step[None, None, :])).astype(F32)
    sel_b = (same_c[None] & (step[None, :, None] == step[None, None, :] - t_i)).astype(F32)
    mt = (jnp.einsum('gcx,txy->gtcy', kmat[0], sel_f, precision=hp)
          + jnp.einsum('gcx,txy->gtcy', kmat[1], sel_b, precision=hp)).reshape(N_GROUPS, CW, CW)

    pt = jnp.stack([wr_re[0], wr_im[0], w_re[1], w_im[1]], axis=1).reshape(N_GROUPS, 2 * SW, CW)

    def carry(d, pr, pi):
        cr = c_re[d][:, None]
        ci = c_im[d][:, None]
        pr = pr.transpose(0, 2, 1)[:, :, None, :]
        pi = pi.transpose(0, 2, 1)[:, :, None, :]
        return [cr * pr - ci * pi, -(cr * pi + ci * pr)]

    qt = jnp.concatenate(carry(0, pw_re[0, ..., 1:], pw_im[0, ..., 1:])
                         + carry(1, pw_re[1, ..., :0:-1], pw_im[1, ..., :0:-1]), axis=-1)
    qt = qt.reshape(N_GROUPS, CW, 2 * SW)

    a_re = pw_re[..., CHUNK]
    a_im = pw_im[..., CHUNK]
    ar = jnp.concatenate([a_re, a_re], axis=-1)
    ai = jnp.concatenate([-a_im, a_im], axis=-1)
    return mt.astype(BF16), pt.astype(BF16), qt.astype(BF16), ar, ai


def _trunk(x, p, tm, nk):
    b, l, _ = x.shape
    nc = b * l // CHUNK
    kc = min(128, l // CHUNK)
    q_t, k, v_t = _qkv_call(x, p['g_mix0'], p['wqkv_t'], p['gq'], p['gk'], tm)
    o_t = _attn_call(q_t, k, v_t, p['hm'], p['bias'], p['rmask'])
    x_ph, h_ph = _attn_ffn_call(x, o_t, p['wo'], p['g_ffn0'], p['wgu0'], p['wd0'], p['g_mix1'], tm)

    sf, sb = _s5_state_call(h_ph, p['pt'], nk)
    sshape = (b, l // CHUNK, N_GROUPS, SW)
    xin_f, xin_b = _scan_call(sf.reshape(sshape), sb.reshape(sshape), p['ar'], p['ai'], kc)
    z_pm = _s5_out_call(h_ph, xin_f.reshape(nc, N_GROUPS, SW), xin_b.reshape(nc, N_GROUPS, SW),
                        p['mt'], p['qt'], p['d_col'], nk)
    y = _glu_ffn_call(x_ph, z_pm, p['wglu'], p['g_ffn1'], p['wgu1'], p['wd1'], tm)
    return y.reshape(b, l, D_MODEL)


def kernel(x_prompt, x_sample, norm_mix, norm_ffn, w_qkv, w_o, q_gain, k_gain, rpb, lam_re, lam_im, log_step,
           b_re, b_im, c_re, c_im, d_skip, w_glu, w_gate_up, w_down):
    tm, nk = 512, 128
    scale = HEAD_DIM ** -0.5 * LOG2E
    mt, pt, qt, ar, ai = _s5_matrices(lam_re[0], lam_im[0], log_step[0], b_re[0], b_im[0], c_re[0], c_im[0])
    bias, rmask = _attn_tables(rpb[0])
    rows = np.arange(PAIR_W)[None, :, None]
    hm = np.broadcast_to((rows // HEAD_DIM) == np.arange(2)[:, None, None], (2, PAIR_W, QB))
    p = dict(
        g_mix0=norm_mix[0].astype(F32)[None], g_mix1=norm_mix[1].astype(F32)[None],
        g_ffn0=norm_ffn[0].astype(F32)[None], g_ffn1=norm_ffn[1].astype(F32)[None],
        wqkv_t=w_qkv[0].T.astype(BF16), wo=w_o[0].astype(BF16),
        gq=jnp.broadcast_to((q_gain[0].astype(F32) * scale)[:, None], (HEAD_DIM, tm // ROW_SPLIT)),
        gk=jnp.broadcast_to(k_gain[0].astype(F32)[:, None], (HEAD_DIM, tm // ROW_SPLIT)),
        hm=jnp.asarray(hm, BF16), bias=bias, rmask=rmask,
        wgu0=w_gate_up[0].astype(BF16), wd0=w_down[0].astype(BF16),
        wgu1=w_gate_up[1].astype(BF16), wd1=w_down[1].astype(BF16),
        wglu=w_glu[0].astype(BF16),
        mt=mt, pt=pt, qt=qt, ar=ar, ai=ai,
        d_col=jnp.broadcast_to(d_skip[0].astype(F32)[:, None], (D_MODEL, nk)),
    )
    return _trunk(x_prompt, p, tm, nk), _trunk(x_sample, p, tm, nk)
```

```python
import functools

import numpy as np
import jax
import jax.numpy as jnp
from jax import lax
from jax.experimental import pallas as pl
from jax.experimental.pallas import tpu as pltpu

D_MODEL = 1024
GRID_W = 64
N_HEADS = 16
HEAD_DIM = D_MODEL // N_HEADS
WIN_ROWS = 8
WIN_COLS = 16
SSM_GROUP = 16
N_GROUPS = D_MODEL // SSM_GROUP
STATE = 64
D_FF = 2816
EPS = 1e-6

F32 = jnp.float32
BF16 = jnp.bfloat16

V7X_LANES = 128
V7X_VMEM_BYTES = 64 * 1024 * 1024
VMEM_LIMIT = (V7X_VMEM_BYTES * 7) // 8

HEAD_PAIRS = N_HEADS // 2
PAIR_W = 2 * HEAD_DIM
Q_ROWS = 4
QB = Q_ROWS * GRID_W
KB = 3 * QB
NEG = -1e30
ONES_ROWS = 16
LOG2E = 1.4426950408889634
RM_ROWS = 16
TILE_SKIP, TILE_OPEN, TILE_MASKED = 0, 1, 2
HEADS_PER_TRIP = 8


def _tile_plans():
    assert 2 * Q_ROWS <= WIN_ROWS and V7X_LANES % GRID_W == 0
    per_tile = V7X_LANES // GRID_W
    interior, edge = [], []
    for kr in range(3 * Q_ROWS):
        irow, erow = [], []
        for c in range(QB // V7X_LANES):
            ok = [r <= kr < r + WIN_ROWS for r in range(per_tile * c, per_tile * (c + 1))]
            irow.append(TILE_OPEN if all(ok) else TILE_MASKED if any(ok) else TILE_SKIP)
            erow.append(TILE_OPEN if Q_ROWS <= kr < 2 * Q_ROWS else TILE_MASKED)
        interior.append(irow)
        edge.append(erow)
    return interior, edge


_INTERIOR_PLAN, _EDGE_PLAN = _tile_plans()

CHUNK = 16
CW = CHUNK * SSM_GROUP
SW = 2 * STATE
GROUP_BLOCK = 16
GB_W = GROUP_BLOCK * SSM_GROUP
N_GB = N_GROUPS // GROUP_BLOCK
ROW_SPLIT = 2


def _cparams(sem):
    return pltpu.CompilerParams(dimension_semantics=sem, vmem_limit_bytes=VMEM_LIMIT)


def _rms(x, g):
    return x * lax.rsqrt(jnp.mean(x * x, axis=-1, keepdims=True) + EPS) * g


def _dot(a, b):
    return jnp.dot(a, b, preferred_element_type=F32)


def _dot_nt(a, b):
    return lax.dot_general(a, b, (((1,), (1,)), ((), ())), preferred_element_type=F32)


def _dot_tn(a, b):
    return lax.dot_general(a, b, (((0,), (0,)), ((), ())), preferred_element_type=F32)


def _qkv_kernel(x_ref, g_ref, w_ref, gq_ref, gk_ref, q_ref, k_ref, v_ref):
    tm = x_ref.shape[1] // ROW_SPLIT

    def head_norm(t, gain):
        t3 = t.reshape(N_HEADS, HEAD_DIM, tm)
        y = t3 * lax.rsqrt(jnp.mean(t3 * t3, axis=1, keepdims=True) + EPS)
        return y * gain[None]

    for r in range(ROW_SPLIT):
        tok = slice(r * tm, (r + 1) * tm)
        h = _rms(x_ref[0, tok, :], g_ref[...]).astype(BF16)
        q = head_norm(_dot_nt(w_ref[0:D_MODEL, :], h), gq_ref[...])
        k = head_norm(_dot_nt(w_ref[D_MODEL:2 * D_MODEL, :], h), gk_ref[...])
        v = _dot_nt(w_ref[2 * D_MODEL:3 * D_MODEL, :], h)
        q_ref[0, r] = q.reshape(HEAD_PAIRS, PAIR_W, tm).astype(BF16)
        v_ref[0, r] = v.reshape(HEAD_PAIRS, PAIR_W, tm).astype(BF16)
        kt = k.reshape(D_MODEL, tm).T
        for j in range(HEAD_PAIRS):
            k_ref[0, j, tok, :] = kt[:, PAIR_W * j:PAIR_W * (j + 1)].astype(BF16)


def _qkv_call(x, g, w_t, gq, gk, tm):
    b, l, _ = x.shape
    assert tm // ROW_SPLIT == QB
    blocked = pl.BlockSpec((1, ROW_SPLIT, HEAD_PAIRS, PAIR_W, QB), lambda i, j: (i, j, 0, 0, 0))
    return pl.pallas_call(
        _qkv_kernel,
        grid=(b, l // tm),
        in_specs=[
            pl.BlockSpec((1, tm, D_MODEL), lambda i, j: (i, j, 0)),
            pl.BlockSpec((1, D_MODEL), lambda i, j: (0, 0)),
            pl.BlockSpec((3 * D_MODEL, D_MODEL), lambda i, j: (0, 0)),
            pl.BlockSpec((HEAD_DIM, tm // ROW_SPLIT), lambda i, j: (0, 0)),
            pl.BlockSpec((HEAD_DIM, tm // ROW_SPLIT), lambda i, j: (0, 0)),
        ],
        out_specs=[
            blocked,
            pl.BlockSpec((1, HEAD_PAIRS, tm, PAIR_W), lambda i, j: (i, 0, j, 0)),
            blocked,
        ],
        out_shape=[
            jax.ShapeDtypeStruct((b, l // QB, HEAD_PAIRS, PAIR_W, QB), BF16),
            jax.ShapeDtypeStruct((b, HEAD_PAIRS, l, PAIR_W), BF16),
            jax.ShapeDtypeStruct((b, l // QB, HEAD_PAIRS, PAIR_W, QB), BF16),
        ],
        compiler_params=_cparams(("parallel", "parallel")),
        name="qkv_proj",
    )(x, g, w_t, gq, gk)


def _attn_kernel(q_ref, kp_ref, kc_ref, kn_ref, vp_ref, vc_ref, vn_ref, hm_ref, bias_ref, rm_ref, o_ref,
                 sa_sc, sb_sc, ma_sc, mb_sc):
    ones = jnp.ones((ONES_ROWS, QB), BF16)
    zero_tile = jnp.zeros((GRID_W, V7X_LANES), BF16)
    lane_tiles = QB // V7X_LANES

    def tile(kr, c):
        return slice(GRID_W * kr, GRID_W * (kr + 1)), slice(V7X_LANES * c, V7X_LANES * (c + 1))

    def scores(j, e, s_sc, m_sc, plan):
        qm = q_ref[0, 0, j] * hm_ref[e]
        mx = [None] * lane_tiles
        for blk, k_ref in enumerate((kp_ref, kc_ref, kn_ref)):
            used = [kl for kl in range(Q_ROWS) if any(t != TILE_SKIP for t in plan[Q_ROWS * blk + kl])]
            k0 = used[0]
            d = _dot(k_ref[0, j, GRID_W * k0:GRID_W * (used[-1] + 1), :], qm)
            for kl in used:
                kr = Q_ROWS * blk + kl
                for c in range(lane_tiles):
                    if plan[kr][c] == TILE_SKIP:
                        continue
                    rows, lanes = tile(kr, c)
                    s = d[GRID_W * (kl - k0):GRID_W * (kl - k0 + 1), lanes] + bias_ref[j, e, rows, lanes]
                    if plan[kr][c] == TILE_MASKED:
                        s = s + rm_ref[0, kr:kr + 1, lanes]
                    s_sc[rows, lanes] = s
                    mx[c] = s if mx[c] is None else jnp.maximum(mx[c], s)
        m_sc[...] = jnp.concatenate(mx, axis=1)

    def finish(j, e, s_sc, m_sc, plan):
        lo, hi = HEAD_DIM * e, HEAD_DIM * (e + 1)
        m = jnp.max(m_sc[...], axis=0, keepdims=True)
        o = None
        for blk, v_ref in enumerate((vp_ref, vc_ref, vn_ref)):
            p_rows = []
            for kl in range(Q_ROWS):
                kr = Q_ROWS * blk + kl
                p_tiles = []
                for c in range(lane_tiles):
                    rows, lanes = tile(kr, c)
                    p_tiles.append(zero_tile if plan[kr][c] == TILE_SKIP else
                                   jnp.exp2(s_sc[rows, lanes] - m[:, lanes]).astype(BF16))
                p_rows.append(jnp.concatenate(p_tiles, axis=1))
            pb = jnp.concatenate(p_rows, axis=0)
            v = jnp.concatenate([v_ref[0, 0, j, lo:hi, :], ones], axis=0)
            ob = _dot(v, pb)
            o = ob if o is None else o + ob
        o_ref[0, 0, j, lo:hi, :] = (o[:HEAD_DIM] / o[HEAD_DIM:HEAD_DIM + 1]).astype(BF16)

    def all_heads(plan):
        bufs = ((sa_sc, ma_sc), (sb_sc, mb_sc))

        def stage(pair0, t):
            scores(pair0 + (t + 1) // 2, (t + 1) % 2, *bufs[(t + 1) % 2], plan)
            finish(pair0 + t // 2, t % 2, *bufs[t % 2], plan)

        scores(0, 0, *bufs[0], plan)

        def body(i, carry):
            for t in range(HEADS_PER_TRIP):
                stage((HEADS_PER_TRIP // 2) * i, t)
            return carry

        trips = (N_HEADS - 1) // HEADS_PER_TRIP
        lax.fori_loop(0, trips, body, 0)
        for t in range(trips * HEADS_PER_TRIP, N_HEADS - 1):
            stage(0, t)
        finish(HEAD_PAIRS - 1, 1, *bufs[1], plan)

    interior = jnp.logical_and(pl.program_id(1) > 0, pl.program_id(1) < pl.num_programs(1) - 1)
    pl.when(interior)(lambda: all_heads(_INTERIOR_PLAN))
    pl.when(jnp.logical_not(interior))(lambda: all_heads(_EDGE_PLAN))


def _attn_call(q_t, k, v_t, hm, bias, rmask):
    b, nblk = q_t.shape[:2]
    prev = lambda j: jnp.maximum(j - 1, 0)
    nxt = lambda j: jnp.minimum(j + 1, nblk - 1)
    variant = lambda j: (j > 0).astype(jnp.int32) + (j == nblk - 1).astype(jnp.int32)
    kspec = lambda f: pl.BlockSpec((1, HEAD_PAIRS, QB, PAIR_W), lambda i, j: (i, 0, f(j), 0))
    vspec = lambda f: pl.BlockSpec((1, 1, HEAD_PAIRS, PAIR_W, QB), lambda i, j: (i, f(j), 0, 0, 0))
    same = lambda j: j
    return pl.pallas_call(
        _attn_kernel,
        grid=(b, nblk),
        in_specs=[
            vspec(same),
            kspec(prev), kspec(same), kspec(nxt),
            vspec(prev), vspec(same), vspec(nxt),
            pl.BlockSpec((2, PAIR_W, QB), lambda i, j: (0, 0, 0)),
            pl.BlockSpec((HEAD_PAIRS, 2, KB, QB), lambda i, j: (0, 0, 0, 0), pipeline_mode=pl.Buffered(1)),
            pl.BlockSpec((1, RM_ROWS, QB), lambda i, j: (variant(j), 0, 0)),
        ],
        out_specs=vspec(same),
        out_shape=jax.ShapeDtypeStruct(q_t.shape, BF16),
        scratch_shapes=[pltpu.VMEM((KB, QB), F32)] * 2 + [pltpu.VMEM((GRID_W, QB), F32)] * 2,
        compiler_params=_cparams(("parallel", "parallel")),
        name="nbr_attention",
    )(q_t, k, k, k, v_t, v_t, v_t, hm, bias, rmask)


def _attn_tables(rpb):
    kr = np.arange(3 * Q_ROWS)[:, None, None, None]
    kc = np.arange(GRID_W)[None, :, None, None]
    r = np.arange(Q_ROWS)[None, None, :, None]
    c = np.arange(GRID_W)[None, None, None, :]
    shape = (3 * Q_ROWS, GRID_W, Q_ROWS, GRID_W)
    cs = np.clip(c - WIN_COLS // 2, 0, GRID_W - WIN_COLS)
    col_ok = ((kc >= cs) & (kc < cs + WIN_COLS))[0, :, 0, :]
    dc = (kc - c + WIN_COLS - 1)[0, :, 0, :]
    onehot = ((dc[None] == np.arange(2 * WIN_COLS - 1)[:, None, None]) & col_ok[None]).astype(np.float32)
    toep = jnp.einsum('hrd,dkc->hrkc', rpb.astype(F32), onehot, precision=lax.Precision.HIGHEST)
    toep = jnp.where(col_ok[None, None], toep * LOG2E, NEG)
    off = WIN_ROWS - 1 - Q_ROWS
    bias = jnp.concatenate([toep[:, off - rr:off - rr + 3 * Q_ROWS] for rr in range(Q_ROWS)], axis=-1)
    bias = bias.reshape(HEAD_PAIRS, 2, KB, QB)
    rshape = (3 * Q_ROWS, 1, Q_ROWS, GRID_W)
    first = np.broadcast_to(kr >= Q_ROWS, rshape)
    inner = np.broadcast_to((kr >= r) & (kr < r + WIN_ROWS), rshape)
    last = np.broadcast_to(kr < 2 * Q_ROWS, rshape)
    rmask = np.where(np.stack([first, inner, last]), 0.0, NEG).astype(np.float32).reshape(3, 3 * Q_ROWS, QB)
    rmask = np.pad(rmask, ((0, 0), (0, RM_ROWS - 3 * Q_ROWS), (0, 0)))
    return bias, jnp.asarray(rmask)


def _ffn(x0, gf_ref, wgu_ref, wd_ref):
    hb = _rms(x0, gf_ref[...]).astype(BF16)
    g = _dot(hb, wgu_ref[:, :D_FF])
    u = _dot(hb, wgu_ref[:, D_FF:])
    act = (g * (1.0 / (1.0 + jnp.exp(-g))) * u).astype(BF16)
    return x0 + _dot(act, wd_ref[...])


def _resident(shape):
    return pl.BlockSpec(shape, lambda *idx: (0,) * len(shape), pipeline_mode=pl.Buffered(1))


def _phase_major(x):
    return jnp.swapaxes(x.reshape(-1, CHUNK, D_MODEL), 0, 1)


def _store_chunk_major(refs, x):
    xs = _phase_major(x)
    for s in range(CHUNK):
        for ref in refs:
            ref[:, D_MODEL * s:D_MODEL * (s + 1)] = xs[s].astype(ref.dtype)


def _attn_ffn_kernel(x_ref, o_ref, wo_ref, gf_ref, wgu_ref, wd_ref, gn_ref, x1_ref, h1_ref, hb_ref):
    parts = []
    for r in range(ROW_SPLIT):
        o = o_ref[0, r].reshape(D_MODEL, QB)
        x0 = x_ref[0, QB * r:QB * (r + 1), :] + _dot_tn(o, wo_ref[...])
        parts.append(_ffn(x0, gf_ref, wgu_ref, wd_ref))
    x1 = jnp.concatenate(parts, axis=0)
    _store_chunk_major([x1_ref], x1)
    _store_chunk_major([h1_ref, hb_ref], _rms(x1, gn_ref[...]))


def _attn_ffn_call(x, o_t, wo, gf, wgu, wd, gn, tm):
    b, l, _ = x.shape
    nt = l // tm
    out = pl.BlockSpec((tm // CHUNK, CHUNK * D_MODEL), lambda i, j: (i * nt + j, 0))
    return pl.pallas_call(
        _attn_ffn_kernel,
        grid=(b, nt),
        in_specs=[
            pl.BlockSpec((1, tm, D_MODEL), lambda i, j: (i, j, 0)),
            pl.BlockSpec((1, ROW_SPLIT, HEAD_PAIRS, PAIR_W, QB), lambda i, j: (i, j, 0, 0, 0)),
            _resident((D_MODEL, D_MODEL)),
            _resident((1, D_MODEL)),
            _resident((D_MODEL, 2 * D_FF)),
            _resident((D_FF, D_MODEL)),
            _resident((1, D_MODEL)),
        ],
        out_specs=[out, out, out],
        out_shape=[jax.ShapeDtypeStruct((b * l // CHUNK, CHUNK * D_MODEL), dt) for dt in (F32, F32, BF16)],
        compiler_params=_cparams(("parallel", "parallel")),
        name="attn_out_ffn",
    )(x, o_t, wo, gf, wgu, wd, gn)


def _glu_ffn_kernel(x_ref, z_ref, wglu_ref, gf_ref, wgu_ref, wd_ref, y_ref):
    parts = []
    for r in range(ROW_SPLIT):
        ph = range(r * CHUNK // ROW_SPLIT, (r + 1) * CHUNK // ROW_SPLIT)
        z = z_ref[ph.start:ph.stop].reshape(-1, D_MODEL)
        x = jnp.concatenate([x_ref[:, D_MODEL * s:D_MODEL * (s + 1)] for s in ph], axis=0)
        zz = _dot(z, wglu_ref[...])
        za, zg = zz[:, :D_MODEL], zz[:, D_MODEL:]
        x0 = x + za * (1.0 / (1.0 + jnp.exp(-zg)))
        parts.append(_ffn(x0, gf_ref, wgu_ref, wd_ref))
    y = jnp.concatenate(parts, axis=0).reshape(CHUNK, -1, D_MODEL)
    y_ref[...] = jnp.swapaxes(y, 0, 1).reshape(-1, D_MODEL)


def _glu_ffn_call(x_ph, z_pm, wglu, gf, wgu, wd, tm):
    nc = x_ph.shape[0]
    nk = tm // CHUNK
    return pl.pallas_call(
        _glu_ffn_kernel,
        grid=(nc // nk,),
        in_specs=[
            pl.BlockSpec((nk, CHUNK * D_MODEL), lambda i: (i, 0)),
            pl.BlockSpec((CHUNK, nk, D_MODEL), lambda i: (0, i, 0)),
            _resident((D_MODEL, 2 * D_MODEL)),
            _resident((1, D_MODEL)),
            _resident((D_MODEL, 2 * D_FF)),
            _resident((D_FF, D_MODEL)),
        ],
        out_specs=pl.BlockSpec((tm, D_MODEL), lambda i: (i, 0)),
        out_shape=jax.ShapeDtypeStruct((nc * CHUNK, D_MODEL), F32),
        compiler_params=_cparams(("parallel",)),
        name="glu_out_ffn",
    )(x_ph, z_pm, wglu, gf, wgu, wd)


def _chunk_operands(h_refs, ut_sc, ht_sc):
    for s, h_ref in enumerate(h_refs):
        ht = h_ref[...].T
        if ht_sc is not None:
            ht_sc[s] = ht
        ut_sc[:, SSM_GROUP * s:SSM_GROUP * (s + 1), :] = (
            ht.reshape(GROUP_BLOCK, SSM_GROUP, -1).astype(BF16))


def _s5_state_kernel(*refs):
    h_refs = refs[:CHUNK]
    pt_ref, sf_ref, sb_ref, ut_sc, st_sc = refs[CHUNK:]
    _chunk_operands(h_refs, ut_sc, None)
    for g in range(GROUP_BLOCK):
        st_sc[g] = _dot(pt_ref[g], ut_sc[g]).T
    sf_ref[...] = jnp.swapaxes(st_sc[:, :, :SW], 0, 1)
    sb_ref[...] = jnp.swapaxes(st_sc[:, :, SW:], 0, 1)


def _h_specs(nk):
    return [pl.BlockSpec((nk, GB_W), lambda gb, i, s=s: (i, s * N_GB + gb)) for s in range(CHUNK)]


def _s5_state_call(h_ph, pt, nk):
    nc = h_ph.shape[0]
    sspec = pl.BlockSpec((nk, GROUP_BLOCK, SW), lambda gb, i: (i, gb, 0))
    return pl.pallas_call(
        _s5_state_kernel,
        grid=(N_GB, nc // nk),
        in_specs=[*_h_specs(nk),
                  pl.BlockSpec((GROUP_BLOCK, 2 * SW, CW), lambda gb, i: (gb, 0, 0))],
        out_specs=[sspec, sspec],
        out_shape=[jax.ShapeDtypeStruct((nc, N_GROUPS, SW), F32)] * 2,
        scratch_shapes=[pltpu.VMEM((GROUP_BLOCK, CW, nk), BF16),
                        pltpu.VMEM((GROUP_BLOCK, nk, 2 * SW), F32)],
        compiler_params=_cparams(("parallel", "parallel")),
        name="s5_chunk_states",
    )(*([h_ph] * CHUNK), pt)


def _scan_kernel(sf_ref, sb_ref, ar_ref, ai_ref, xf_ref, xb_ref, st_sc):
    kc = sf_ref.shape[1]

    @pl.when(pl.program_id(1) == 0)
    def _():
        st_sc[...] = jnp.zeros_like(st_sc)

    a = [(ar_ref[d], ai_ref[d]) for d in range(2)]

    def step(d, s_ref, x_ref, k, x, xs):
        ar, ai = a[d]
        s = s_ref[0, k]
        x_ref[0, k] = x.astype(BF16)
        return ar * x + ai * xs + s, ar * xs - ai * x + pltpu.roll(s, STATE, 1)

    def body(i, carry):
        xf, xfs, xb, xbs = carry
        return (*step(0, sf_ref, xf_ref, i, xf, xfs), *step(1, sb_ref, xb_ref, kc - 1 - i, xb, xbs))

    init = (st_sc[0, 0], st_sc[0, 1], st_sc[1, 0], st_sc[1, 1])
    xf, xfs, xb, xbs = lax.fori_loop(0, kc, body, init, unroll=8)
    st_sc[0, 0], st_sc[0, 1], st_sc[1, 0], st_sc[1, 1] = xf, xfs, xb, xbs


def _scan_call(sf, sb, ar, ai, kc):
    b, ncs = sf.shape[:2]
    nkb = ncs // kc
    fwd = pl.BlockSpec((1, kc, N_GROUPS, SW), lambda i, j: (i, j, 0, 0))
    bwd = pl.BlockSpec((1, kc, N_GROUPS, SW), lambda i, j: (i, nkb - 1 - j, 0, 0))
    tab = pl.BlockSpec((2, N_GROUPS, SW), lambda i, j: (0, 0, 0))
    return pl.pallas_call(
        _scan_kernel,
        grid=(b, nkb),
        in_specs=[fwd, bwd, tab, tab],
        out_specs=[fwd, bwd],
        out_shape=[jax.ShapeDtypeStruct(sf.shape, BF16)] * 2,
        scratch_shapes=[pltpu.VMEM((2, 2, N_GROUPS, SW), F32)],
        compiler_params=_cparams(("parallel", "arbitrary")),
        name="s5_chunk_scan",
    )(sf, sb, ar, ai)


def _s5_out_kernel(*refs):
    h_refs = refs[:CHUNK]
    xf_ref, xb_ref, mt_ref, qt_ref, d_ref, z_ref, ut_sc, ht_sc, yt_sc = refs[CHUNK:]
    _chunk_operands(h_refs, ut_sc, ht_sc)
    xf = jnp.swapaxes(xf_ref[...].astype(F32), 0, 1)
    xb = jnp.swapaxes(xb_ref[...].astype(F32), 0, 1)
    for g in range(GROUP_BLOCK):
        xin = jnp.concatenate([xf[g], xb[g]], axis=1).astype(BF16)
        yt_sc[g] = _dot(mt_ref[g], ut_sc[g]) + _dot_nt(qt_ref[g], xin)
    d = d_ref[...]
    for t in range(CHUNK):
        y = yt_sc[:, SSM_GROUP * t:SSM_GROUP * (t + 1), :].reshape(GB_W, -1)
        v = y + d * ht_sc[t]
        z_ref[t] = jax.nn.gelu(v, approximate=True).T.astype(BF16)


def _s5_out_call(h_ph, xin_f, xin_b, mt, qt, d_col, nk):
    nc = h_ph.shape[0]
    xspec = pl.BlockSpec((nk, GROUP_BLOCK, SW), lambda gb, i: (i, gb, 0))
    wspec = pl.BlockSpec((GROUP_BLOCK, CW, CW), lambda gb, i: (gb, 0, 0))
    return pl.pallas_call(
        _s5_out_kernel,
        grid=(N_GB, nc // nk),
        in_specs=[*_h_specs(nk), xspec, xspec, wspec, wspec,
                  pl.BlockSpec((GB_W, nk), lambda gb, i: (gb, 0))],
        out_specs=pl.BlockSpec((CHUNK, nk, GB_W), lambda gb, i: (0, i, gb)),
        out_shape=jax.ShapeDtypeStruct((CHUNK, nc, D_MODEL), BF16),
        scratch_shapes=[
            pltpu.VMEM((GROUP_BLOCK, CW, nk), BF16),
            pltpu.VMEM((CHUNK, GB_W, nk), F32),
            pltpu.VMEM((GROUP_BLOCK, CW, nk), F32),
        ],
        compiler_params=_cparams(("parallel", "parallel")),
        name="s5_outputs",
    )(*([h_ph] * CHUNK), xin_f, xin_b, mt, qt, d_col)


def _s5_matrices(lam_re, lam_im, log_step, b_re, b_im, c_re, c_im):
    hp = lax.Precision.HIGHEST
    dt = jnp.exp(log_step.astype(F32))[..., None]
    lam_re = lam_re.astype(F32)
    lam_im = lam_im.astype(F32)
    mag = jnp.exp(lam_re * dt)
    ang = lam_im * dt
    lb_re = mag * jnp.cos(ang)
    lb_im = mag * jnp.sin(ang)
    den = lam_re * lam_re + lam_im * lam_im
    nr = lb_re - 1.0
    ni = lb_im
    coef_re = (nr * lam_re + ni * lam_im) / den
    coef_im = (ni * lam_re - nr * lam_im) / den
    b_re = b_re.astype(F32)
    b_im = b_im.astype(F32)
    bb_re = coef_re[..., None] * b_re - coef_im[..., None] * b_im
    bb_im = coef_re[..., None] * b_im + coef_im[..., None] * b_re
    c_re = c_re.astype(F32)
    c_im = c_im.astype(F32)

    pw_re = [jnp.ones_like(lb_re)]
    pw_im = [jnp.zeros_like(lb_im)]
    for _ in range(CHUNK):
        pr, pi = pw_re[-1], pw_im[-1]
        pw_re.append(pr * lb_re - pi * lb_im)
        pw_im.append(pr * lb_im + pi * lb_re)
    pw_re = jnp.stack(pw_re, axis=-1)
    pw_im = jnp.stack(pw_im, axis=-1)

    bbl_re = jnp.tile(bb_re, (1, 1, 1, CHUNK))
    bbl_im = jnp.tile(bb_im, (1, 1, 1, CHUNK))

    def response(pr, pi):
        prl = jnp.repeat(pr, SSM_GROUP, axis=-1)
        pil = jnp.repeat(pi, SSM_GROUP, axis=-1)
        return prl * bbl_re - pil * bbl_im, prl * bbl_im + pil * bbl_re

    w_re, w_im = response(pw_re[..., :CHUNK], pw_im[..., :CHUNK])
    wr_re, wr_im = response(pw_re[..., CHUNK - 1::-1], pw_im[..., CHUNK - 1::-1])

    kmat = (jnp.einsum('dgcn,dgnx->dgcx', c_re, w_re, precision=hp)
            - jnp.einsum('dgcn,dgnx->dgcx', c_im, w_im, precision=hp))
    lane = jnp.arange(CW)
    step, chan = lane // SSM_GROUP, lane % SSM_GROUP
    same_c = chan[:, None] == chan[None, :]
    t_i = jnp.arange(CHUNK)[:, None, None]
    sel_f = (same_c[None] & (step[None, :, None] == t_i - step[None, None, :])).astype(F32)
    sel_b = (same_c[None] & (step[None, :, None] == step[None, None, :] - t_i)).astype(F32)
    mt = (jnp.einsum('gcx,txy->gtcy', kmat[0], sel_f, precision=hp)
          + jnp.einsum('gcx,txy->gtcy', kmat[1], sel_b, precision=hp)).reshape(N_GROUPS, CW, CW)

    pt = jnp.stack([wr_re[0], wr_im[0], w_re[1], w_im[1]], axis=1).reshape(N_GROUPS, 2 * SW, CW)

    def carry(d, pr, pi):
        cr = c_re[d][:, None]
        ci = c_im[d][:, None]
        pr = pr.transpose(0, 2, 1)[:, :, None, :]
        pi = pi.transpose(0, 2, 1)[:, :, None, :]
        return [cr * pr - ci * pi, -(cr * pi + ci * pr)]

    qt = jnp.concatenate(carry(0, pw_re[0, ..., 1:], pw_im[0, ..., 1:])
                         + carry(1, pw_re[1, ..., :0:-1], pw_im[1, ..., :0:-1]), axis=-1)
    qt = qt.reshape(N_GROUPS, CW, 2 * SW)

    a_re = pw_re[..., CHUNK]
    a_im = pw_im[..., CHUNK]
    ar = jnp.concatenate([a_re, a_re], axis=-1)
    ai = jnp.concatenate([-a_im, a_im], axis=-1)
    return mt.astype(BF16), pt.astype(BF16), qt.astype(BF16), ar, ai


def _trunk(x, p, tm, nk):
    b, l, _ = x.shape
    nc = b * l // CHUNK
    kc = min(128, l // CHUNK)
    q_t, k, v_t = _qkv_call(x, p['g_mix0'], p['wqkv_t'], p['gq'], p['gk'], tm)
    o_t = _attn_call(q_t, k, v_t, p['hm'], p['bias'], p['rmask'])
    x_ph, h_ph, hb_ph = _attn_ffn_call(x, o_t, p['wo'], p['g_ffn0'], p['wgu0'], p['wd0'], p['g_mix1'], tm)

    sf, sb = _s5_state_call(hb_ph, p['pt'], nk)
    sshape = (b, l // CHUNK, N_GROUPS, SW)
    xin_f, xin_b = _scan_call(sf.reshape(sshape), sb.reshape(sshape), p['ar'], p['ai'], kc)
    z_pm = _s5_out_call(h_ph, xin_f.reshape(nc, N_GROUPS, SW), xin_b.reshape(nc, N_GROUPS, SW),
                        p['mt'], p['qt'], p['d_col'], nk)
    y = _glu_ffn_call(x_ph, z_pm, p['wglu'], p['g_ffn1'], p['wgu1'], p['wd1'], tm)
    return y.reshape(b, l, D_MODEL)


def kernel(x_prompt, x_sample, norm_mix, norm_ffn, w_qkv, w_o, q_gain, k_gain, rpb, lam_re, lam_im, log_step,
           b_re, b_im, c_re, c_im, d_skip, w_glu, w_gate_up, w_down):
    tm, nk = 512, 128
    scale = HEAD_DIM ** -0.5 * LOG2E
    mt, pt, qt, ar, ai = _s5_matrices(lam_re[0], lam_im[0], log_step[0], b_re[0], b_im[0], c_re[0], c_im[0])
    bias, rmask = _attn_tables(rpb[0])
    rows = np.arange(PAIR_W)[None, :, None]
    hm = np.broadcast_to((rows // HEAD_DIM) == np.arange(2)[:, None, None], (2, PAIR_W, QB))
    p = dict(
        g_mix0=norm_mix[0].astype(F32)[None], g_mix1=norm_mix[1].astype(F32)[None],
        g_ffn0=norm_ffn[0].astype(F32)[None], g_ffn1=norm_ffn[1].astype(F32)[None],
        wqkv_t=w_qkv[0].T.astype(BF16), wo=w_o[0].astype(BF16),
        gq=jnp.broadcast_to((q_gain[0].astype(F32) * scale)[:, None], (HEAD_DIM, tm // ROW_SPLIT)),
        gk=jnp.broadcast_to(k_gain[0].astype(F32)[:, None], (HEAD_DIM, tm // ROW_SPLIT)),
        hm=jnp.asarray(hm, BF16), bias=bias, rmask=rmask,
        wgu0=w_gate_up[0].astype(BF16), wd0=w_down[0].astype(BF16),
        wgu1=w_gate_up[1].astype(BF16), wd1=w_down[1].astype(BF16),
        wglu=w_glu[0].astype(BF16),
        mt=mt, pt=pt, qt=qt, ar=ar, ai=ai,
        d_col=jnp.broadcast_to(d_skip[0].astype(F32)[:, None], (D_MODEL, nk)),
    )
    return _trunk(x_prompt, p, tm, nk), _trunk(x_sample, p, tm, nk)
```

```python
import functools

import numpy as np
import jax
import jax.numpy as jnp
from jax import lax
from jax.experimental import pallas as pl
from jax.experimental.pallas import tpu as pltpu

D_MODEL = 1024
GRID_W = 64
N_HEADS = 16
HEAD_DIM = D_MODEL // N_HEADS
WIN_ROWS = 8
WIN_COLS = 16
SSM_GROUP = 16
N_GROUPS = D_MODEL // SSM_GROUP
STATE = 64
D_FF = 2816
EPS = 1e-6

F32 = jnp.float32
BF16 = jnp.bfloat16

V7X_LANES = 128
V7X_VMEM_BYTES = 64 * 1024 * 1024
VMEM_LIMIT = (V7X_VMEM_BYTES * 7) // 8

HEAD_PAIRS = N_HEADS // 2
PAIR_W = 2 * HEAD_DIM
Q_ROWS = 4
QB = Q_ROWS * GRID_W
KB = 3 * QB
NEG = -1e30
ONES_ROWS = 16
LOG2E = 1.4426950408889634
RM_ROWS = 16
TILE_SKIP, TILE_OPEN, TILE_MASKED = 0, 1, 2
HEADS_PER_TRIP = 8


def _tile_plans():
    assert 2 * Q_ROWS <= WIN_ROWS and V7X_LANES % GRID_W == 0
    per_tile = V7X_LANES // GRID_W
    interior, edge = [], []
    for kr in range(3 * Q_ROWS):
        irow, erow = [], []
        for c in range(QB // V7X_LANES):
            ok = [r <= kr < r + WIN_ROWS for r in range(per_tile * c, per_tile * (c + 1))]
            irow.append(TILE_OPEN if all(ok) else TILE_MASKED if any(ok) else TILE_SKIP)
            erow.append(TILE_OPEN if Q_ROWS <= kr < 2 * Q_ROWS else TILE_MASKED)
        interior.append(irow)
        edge.append(erow)
    return interior, edge


_INTERIOR_PLAN, _EDGE_PLAN = _tile_plans()

CHUNK = 16
CW = CHUNK * SSM_GROUP
SW = 2 * STATE
GROUP_BLOCK = 16
GB_W = GROUP_BLOCK * SSM_GROUP
N_GB = N_GROUPS // GROUP_BLOCK
ROW_SPLIT = 2


def _cparams(sem):
    return pltpu.CompilerParams(dimension_semantics=sem, vmem_limit_bytes=VMEM_LIMIT)


def _rms(x, g):
    return x * lax.rsqrt(jnp.mean(x * x, axis=-1, keepdims=True) + EPS) * g


def _dot(a, b):
    return jnp.dot(a, b, preferred_element_type=F32)


def _dot_nt(a, b):
    return lax.dot_general(a, b, (((1,), (1,)), ((), ())), preferred_element_type=F32)


def _dot_tn(a, b):
    return lax.dot_general(a, b, (((0,), (0,)), ((), ())), preferred_element_type=F32)


def _qkv_kernel(x_ref, g_ref, w_ref, gq_ref, gk_ref, q_ref, k_ref, v_ref):
    tm = x_ref.shape[1] // ROW_SPLIT

    def head_norm(t, gain):
        t3 = t.reshape(N_HEADS, HEAD_DIM, tm)
        y = t3 * lax.rsqrt(jnp.mean(t3 * t3, axis=1, keepdims=True) + EPS)
        return y * gain[None]

    for r in range(ROW_SPLIT):
        tok = slice(r * tm, (r + 1) * tm)
        h = _rms(x_ref[0, tok, :], g_ref[...]).astype(BF16)
        q = head_norm(_dot_nt(w_ref[0:D_MODEL, :], h), gq_ref[...])
        k = head_norm(_dot_nt(w_ref[D_MODEL:2 * D_MODEL, :], h), gk_ref[...])
        v = _dot_nt(w_ref[2 * D_MODEL:3 * D_MODEL, :], h)
        q_ref[0, r] = q.reshape(HEAD_PAIRS, PAIR_W, tm).astype(BF16)
        v_ref[0, r] = v.reshape(HEAD_PAIRS, PAIR_W, tm).astype(BF16)
        kt = k.reshape(D_MODEL, tm).T
        for j in range(HEAD_PAIRS):
            k_ref[0, j, tok, :] = kt[:, PAIR_W * j:PAIR_W * (j + 1)].astype(BF16)


def _qkv_call(x, g, w_t, gq, gk, tm):
    b, l, _ = x.shape
    assert tm // ROW_SPLIT == QB
    blocked = pl.BlockSpec((1, ROW_SPLIT, HEAD_PAIRS, PAIR_W, QB), lambda i, j: (i, j, 0, 0, 0))
    return pl.pallas_call(
        _qkv_kernel,
        grid=(b, l // tm),
        in_specs=[
            pl.BlockSpec((1, tm, D_MODEL), lambda i, j: (i, j, 0)),
            pl.BlockSpec((1, D_MODEL), lambda i, j: (0, 0)),
            pl.BlockSpec((3 * D_MODEL, D_MODEL), lambda i, j: (0, 0)),
            pl.BlockSpec((HEAD_DIM, tm // ROW_SPLIT), lambda i, j: (0, 0)),
            pl.BlockSpec((HEAD_DIM, tm // ROW_SPLIT), lambda i, j: (0, 0)),
        ],
        out_specs=[
            blocked,
            pl.BlockSpec((1, HEAD_PAIRS, tm, PAIR_W), lambda i, j: (i, 0, j, 0)),
            blocked,
        ],
        out_shape=[
            jax.ShapeDtypeStruct((b, l // QB, HEAD_PAIRS, PAIR_W, QB), BF16),
            jax.ShapeDtypeStruct((b, HEAD_PAIRS, l, PAIR_W), BF16),
            jax.ShapeDtypeStruct((b, l // QB, HEAD_PAIRS, PAIR_W, QB), BF16),
        ],
        compiler_params=_cparams(("parallel", "parallel")),
        name="qkv_proj",
    )(x, g, w_t, gq, gk)


def _attn_kernel(q_ref, kp_ref, kc_ref, kn_ref, vp_ref, vc_ref, vn_ref, hm_ref, bias_ref, rm_ref, o_ref,
                 sa_sc, sb_sc, ma_sc, mb_sc):
    ones = jnp.ones((ONES_ROWS, QB), BF16)
    zero_tile = jnp.zeros((GRID_W, V7X_LANES), BF16)
    lane_tiles = QB // V7X_LANES

    def tile(kr, c):
        return slice(GRID_W * kr, GRID_W * (kr + 1)), slice(V7X_LANES * c, V7X_LANES * (c + 1))

    def scores(j, e, s_sc, m_sc, plan):
        qm = q_ref[0, 0, j] * hm_ref[e]
        mx = [None] * lane_tiles
        for blk, k_ref in enumerate((kp_ref, kc_ref, kn_ref)):
            used = [kl for kl in range(Q_ROWS) if any(t != TILE_SKIP for t in plan[Q_ROWS * blk + kl])]
            k0 = used[0]
            d = _dot(k_ref[0, j, GRID_W * k0:GRID_W * (used[-1] + 1), :], qm)
            for kl in used:
                kr = Q_ROWS * blk + kl
                for c in range(lane_tiles):
                    if plan[kr][c] == TILE_SKIP:
                        continue
                    rows, lanes = tile(kr, c)
                    s = d[GRID_W * (kl - k0):GRID_W * (kl - k0 + 1), lanes] + bias_ref[j, e, rows, lanes]
                    if plan[kr][c] == TILE_MASKED:
                        s = s + rm_ref[0, kr:kr + 1, lanes]
                    s_sc[rows, lanes] = s
                    mx[c] = s if mx[c] is None else jnp.maximum(mx[c], s)
        m_sc[...] = jnp.concatenate(mx, axis=1)

    def finish(j, e, s_sc, m_sc, plan):
        lo, hi = HEAD_DIM * e, HEAD_DIM * (e + 1)
        m = jnp.max(m_sc[...], axis=0, keepdims=True)
        o = None
        for blk, v_ref in enumerate((vp_ref, vc_ref, vn_ref)):
            p_rows = []
            for kl in range(Q_ROWS):
                kr = Q_ROWS * blk + kl
                p_tiles = []
                for c in range(lane_tiles):
                    rows, lanes = tile(kr, c)
                    p_tiles.append(zero_tile if plan[kr][c] == TILE_SKIP else
                                   jnp.exp2(s_sc[rows, lanes] - m[:, lanes]).astype(BF16))
                p_rows.append(jnp.concatenate(p_tiles, axis=1))
            pb = jnp.concatenate(p_rows, axis=0)
            v = jnp.concatenate([v_ref[0, 0, j, lo:hi, :], ones], axis=0)
            ob = _dot(v, pb)
            o = ob if o is None else o + ob
        o_ref[0, 0, j, lo:hi, :] = (o[:HEAD_DIM] / o[HEAD_DIM:HEAD_DIM + 1]).astype(BF16)

    def all_heads(plan):
        bufs = ((sa_sc, ma_sc), (sb_sc, mb_sc))

        def stage(pair0, t):
            scores(pair0 + (t + 1) // 2, (t + 1) % 2, *bufs[(t + 1) % 2], plan)
            finish(pair0 + t // 2, t % 2, *bufs[t % 2], plan)

        scores(0, 0, *bufs[0], plan)

        def body(i, carry):
            for t in range(HEADS_PER_TRIP):
                stage((HEADS_PER_TRIP // 2) * i, t)
            return carry

        trips = (N_HEADS - 1) // HEADS_PER_TRIP
        lax.fori_loop(0, trips, body, 0)
        for t in range(trips * HEADS_PER_TRIP, N_HEADS - 1):
            stage(0, t)
        finish(HEAD_PAIRS - 1, 1, *bufs[1], plan)

    interior = jnp.logical_and(pl.program_id(1) > 0, pl.program_id(1) < pl.num_programs(1) - 1)
    pl.when(interior)(lambda: all_heads(_INTERIOR_PLAN))
    pl.when(jnp.logical_not(interior))(lambda: all_heads(_EDGE_PLAN))


def _attn_call(q_t, k, v_t, hm, bias, rmask):
    b, nblk = q_t.shape[:2]
    prev = lambda j: jnp.maximum(j - 1, 0)
    nxt = lambda j: jnp.minimum(j + 1, nblk - 1)
    variant = lambda j: (j > 0).astype(jnp.int32) + (j == nblk - 1).astype(jnp.int32)
    kspec = lambda f: pl.BlockSpec((1, HEAD_PAIRS, QB, PAIR_W), lambda i, j: (i, 0, f(j), 0))
    vspec = lambda f: pl.BlockSpec((1, 1, HEAD_PAIRS, PAIR_W, QB), lambda i, j: (i, f(j), 0, 0, 0))
    same = lambda j: j
    return pl.pallas_call(
        _attn_kernel,
        grid=(b, nblk),
        in_specs=[
            vspec(same),
            kspec(prev), kspec(same), kspec(nxt),
            vspec(prev), vspec(same), vspec(nxt),
            pl.BlockSpec((2, PAIR_W, QB), lambda i, j: (0, 0, 0)),
            pl.BlockSpec((HEAD_PAIRS, 2, KB, QB), lambda i, j: (0, 0, 0, 0), pipeline_mode=pl.Buffered(1)),
            pl.BlockSpec((1, RM_ROWS, QB), lambda i, j: (variant(j), 0, 0)),
        ],
        out_specs=vspec(same),
        out_shape=jax.ShapeDtypeStruct(q_t.shape, BF16),
        scratch_shapes=[pltpu.VMEM((KB, QB), F32)] * 2 + [pltpu.VMEM((GRID_W, QB), F32)] * 2,
        compiler_params=_cparams(("parallel", "parallel")),
        name="nbr_attention",
    )(q_t, k, k, k, v_t, v_t, v_t, hm, bias, rmask)


def _attn_tables(rpb):
    kr = np.arange(3 * Q_ROWS)[:, None, None, None]
    kc = np.arange(GRID_W)[None, :, None, None]
    r = np.arange(Q_ROWS)[None, None, :, None]
    c = np.arange(GRID_W)[None, None, None, :]
    shape = (3 * Q_ROWS, GRID_W, Q_ROWS, GRID_W)
    cs = np.clip(c - WIN_COLS // 2, 0, GRID_W - WIN_COLS)
    col_ok = ((kc >= cs) & (kc < cs + WIN_COLS))[0, :, 0, :]
    dc = (kc - c + WIN_COLS - 1)[0, :, 0, :]
    onehot = ((dc[None] == np.arange(2 * WIN_COLS - 1)[:, None, None]) & col_ok[None]).astype(np.float32)
    toep = jnp.einsum('hrd,dkc->hrkc', rpb.astype(F32), onehot, precision=lax.Precision.HIGHEST)
    toep = jnp.where(col_ok[None, None], toep * LOG2E, NEG)
    off = WIN_ROWS - 1 - Q_ROWS
    bias = jnp.concatenate([toep[:, off - rr:off - rr + 3 * Q_ROWS] for rr in range(Q_ROWS)], axis=-1)
    bias = bias.reshape(HEAD_PAIRS, 2, KB, QB)
    rshape = (3 * Q_ROWS, 1, Q_ROWS, GRID_W)
    first = np.broadcast_to(kr >= Q_ROWS, rshape)
    inner = np.broadcast_to((kr >= r) & (kr < r + WIN_ROWS), rshape)
    last = np.broadcast_to(kr < 2 * Q_ROWS, rshape)
    rmask = np.where(np.stack([first, inner, last]), 0.0, NEG).astype(np.float32).reshape(3, 3 * Q_ROWS, QB)
    rmask = np.pad(rmask, ((0, 0), (0, RM_ROWS - 3 * Q_ROWS), (0, 0)))
    return bias, jnp.asarray(rmask)


def _ffn(x0, gf_ref, wgu_ref, wd_ref):
    hb = _rms(x0, gf_ref[...]).astype(BF16)
    g = _dot(hb, wgu_ref[:, :D_FF])
    u = _dot(hb, wgu_ref[:, D_FF:])
    act = (g * (1.0 / (1.0 + jnp.exp(-g))) * u).astype(BF16)
    return x0 + _dot(act, wd_ref[...])


def _resident(shape):
    return pl.BlockSpec(shape, lambda *idx: (0,) * len(shape), pipeline_mode=pl.Buffered(1))


def _phase_major(x):
    return jnp.swapaxes(x.reshape(-1, CHUNK, D_MODEL), 0, 1)


def _store_chunk_major(refs, x):
    xs = _phase_major(x)
    for s in range(CHUNK):
        for ref in refs:
            ref[:, D_MODEL * s:D_MODEL * (s + 1)] = xs[s].astype(ref.dtype)


def _attn_ffn_kernel(x_ref, o_ref, wo_ref, gf_ref, wgu_ref, wd_ref, gn_ref, x1_ref, h1_ref, hb_ref):
    parts = []
    for r in range(ROW_SPLIT):
        o = o_ref[0, r].reshape(D_MODEL, QB)
        x0 = x_ref[0, QB * r:QB * (r + 1), :] + _dot_tn(o, wo_ref[...])
        parts.append(_ffn(x0, gf_ref, wgu_ref, wd_ref))
    x1 = jnp.concatenate(parts, axis=0)
    _store_chunk_major([x1_ref], x1)
    _store_chunk_major([h1_ref, hb_ref], _rms(x1, gn_ref[...]))


def _attn_ffn_call(x, o_t, wo, gf, wgu, wd, gn, tm):
    b, l, _ = x.shape
    nt = l // tm
    out = pl.BlockSpec((tm // CHUNK, CHUNK * D_MODEL), lambda i, j: (i * nt + j, 0))
    return pl.pallas_call(
        _attn_ffn_kernel,
        grid=(b, nt),
        in_specs=[
            pl.BlockSpec((1, tm, D_MODEL), lambda i, j: (i, j, 0)),
            pl.BlockSpec((1, ROW_SPLIT, HEAD_PAIRS, PAIR_W, QB), lambda i, j: (i, j, 0, 0, 0)),
            _resident((D_MODEL, D_MODEL)),
            _resident((1, D_MODEL)),
            _resident((D_MODEL, 2 * D_FF)),
            _resident((D_FF, D_MODEL)),
            _resident((1, D_MODEL)),
        ],
        out_specs=[out, out, out],
        out_shape=[jax.ShapeDtypeStruct((b * l // CHUNK, CHUNK * D_MODEL), dt) for dt in (F32, F32, BF16)],
        compiler_params=_cparams(("parallel", "parallel")),
        name="attn_out_ffn",
    )(x, o_t, wo, gf, wgu, wd, gn)


def _glu_ffn_kernel(x_ref, z_ref, wglu_ref, gf_ref, wgu_ref, wd_ref, y_ref):
    parts = []
    for r in range(ROW_SPLIT):
        ph = range(r * CHUNK // ROW_SPLIT, (r + 1) * CHUNK // ROW_SPLIT)
        z = z_ref[ph.start:ph.stop].reshape(-1, D_MODEL)
        x = jnp.concatenate([x_ref[:, D_MODEL * s:D_MODEL * (s + 1)] for s in ph], axis=0)
        zz = _dot(z, wglu_ref[...])
        za, zg = zz[:, :D_MODEL], zz[:, D_MODEL:]
        x0 = x + za * (1.0 / (1.0 + jnp.exp(-zg)))
        parts.append(_ffn(x0, gf_ref, wgu_ref, wd_ref))
    y = jnp.concatenate(parts, axis=0).reshape(CHUNK, -1, D_MODEL)
    y_ref[...] = jnp.swapaxes(y, 0, 1).reshape(-1, D_MODEL)


def _glu_ffn_call(x_ph, z_pm, wglu, gf, wgu, wd, tm):
    nc = x_ph.shape[0]
    nk = tm // CHUNK
    return pl.pallas_call(
        _glu_ffn_kernel,
        grid=(nc // nk,),
        in_specs=[
            pl.BlockSpec((nk, CHUNK * D_MODEL), lambda i: (i, 0)),
            pl.BlockSpec((CHUNK, nk, D_MODEL), lambda i: (0, i, 0)),
            _resident((D_MODEL, 2 * D_MODEL)),
            _resident((1, D_MODEL)),
            _resident((D_MODEL, 2 * D_FF)),
            _resident((D_FF, D_MODEL)),
        ],
        out_specs=pl.BlockSpec((tm, D_MODEL), lambda i: (i, 0)),
        out_shape=jax.ShapeDtypeStruct((nc * CHUNK, D_MODEL), F32),
        compiler_params=_cparams(("parallel",)),
        name="glu_out_ffn",
    )(x_ph, z_pm, wglu, gf, wgu, wd)


def _chunk_operands(h_refs, ut_sc, ht_sc):
    for s, h_ref in enumerate(h_refs):
        ht = h_ref[...].T
        if ht_sc is not None:
            ht_sc[s] = ht
        ut_sc[:, SSM_GROUP * s:SSM_GROUP * (s + 1), :] = (
            ht.reshape(GROUP_BLOCK, SSM_GROUP, -1).astype(BF16))


def _s5_state_kernel(*refs):
    h_refs = refs[:CHUNK]
    pt_ref, sf_ref, sb_ref, ut_sc, st_sc = refs[CHUNK:]
    _chunk_operands(h_refs, ut_sc, None)
    for g in range(GROUP_BLOCK):
        st_sc[g] = _dot(pt_ref[g], ut_sc[g]).T
    sf_ref[...] = jnp.swapaxes(st_sc[:, :, :SW], 0, 1)
    sb_ref[...] = jnp.swapaxes(st_sc[:, :, SW:], 0, 1)


def _h_specs(nk):
    return [pl.BlockSpec((nk, GB_W), lambda gb, i, s=s: (i, s * N_GB + gb)) for s in range(CHUNK)]


def _s5_state_call(h_ph, pt, nk):
    nc = h_ph.shape[0]
    sspec = pl.BlockSpec((nk, GROUP_BLOCK, SW), lambda gb, i: (i, gb, 0))
    return pl.pallas_call(
        _s5_state_kernel,
        grid=(N_GB, nc // nk),
        in_specs=[*_h_specs(nk),
                  pl.BlockSpec((GROUP_BLOCK, 2 * SW, CW), lambda gb, i: (gb, 0, 0))],
        out_specs=[sspec, sspec],
        out_shape=[jax.ShapeDtypeStruct((nc, N_GROUPS, SW), F32)] * 2,
        scratch_shapes=[pltpu.VMEM((GROUP_BLOCK, CW, nk), BF16),
                        pltpu.VMEM((GROUP_BLOCK, nk, 2 * SW), F32)],
        compiler_params=_cparams(("parallel", "parallel")),
        name="s5_chunk_states",
    )(*([h_ph] * CHUNK), pt)


def _scan_kernel(sf_ref, sb_ref, ar_ref, ai_ref, xf_ref, xb_ref, st_sc):
    kc = sf_ref.shape[1]

    @pl.when(pl.program_id(1) == 0)
    def _():
        st_sc[...] = jnp.zeros_like(st_sc)

    a = [(ar_ref[d], ai_ref[d]) for d in range(2)]

    def step(d, s_ref, x_ref, k, x, xs):
        ar, ai = a[d]
        s = s_ref[0, k]
        x_ref[0, k] = x.astype(BF16)
        return ar * x + ai * xs + s, ar * xs - ai * x + pltpu.roll(s, STATE, 1)

    def body(i, carry):
        xf, xfs, xb, xbs = carry
        return (*step(0, sf_ref, xf_ref, i, xf, xfs), *step(1, sb_ref, xb_ref, kc - 1 - i, xb, xbs))

    init = (st_sc[0, 0], st_sc[0, 1], st_sc[1, 0], st_sc[1, 1])
    xf, xfs, xb, xbs = lax.fori_loop(0, kc, body, init, unroll=8)
    st_sc[0, 0], st_sc[0, 1], st_sc[1, 0], st_sc[1, 1] = xf, xfs, xb, xbs


def _scan_call(sf, sb, ar, ai, kc):
    b, ncs = sf.shape[:2]
    nkb = ncs // kc
    fwd = pl.BlockSpec((1, kc, N_GROUPS, SW), lambda i, j: (i, j, 0, 0))
    bwd = pl.BlockSpec((1, kc, N_GROUPS, SW), lambda i, j: (i, nkb - 1 - j, 0, 0))
    tab = pl.BlockSpec((2, N_GROUPS, SW), lambda i, j: (0, 0, 0))
    return pl.pallas_call(
        _scan_kernel,
        grid=(b, nkb),
        in_specs=[fwd, bwd, tab, tab],
        out_specs=[fwd, bwd],
        out_shape=[jax.ShapeDtypeStruct(sf.shape, BF16)] * 2,
        scratch_shapes=[pltpu.VMEM((2, 2, N_GROUPS, SW), F32)],
        compiler_params=_cparams(("parallel", "arbitrary")),
        name="s5_chunk_scan",
    )(sf, sb, ar, ai)


def _s5_out_kernel(*refs):
    h_refs = refs[:CHUNK]
    xf_ref, xb_ref, mt_ref, qt_ref, d_ref, z_ref, ut_sc, ht_sc, yt_sc = refs[CHUNK:]
    _chunk_operands(h_refs, ut_sc, ht_sc)
    xf = jnp.swapaxes(xf_ref[...].astype(F32), 0, 1)
    xb = jnp.swapaxes(xb_ref[...].astype(F32), 0, 1)
    for g in range(GROUP_BLOCK):
        xin = jnp.concatenate([xf[g], xb[g]], axis=1).astype(BF16)
        yt_sc[g] = _dot(mt_ref[g], ut_sc[g]) + _dot_nt(qt_ref[g], xin)
    d = d_ref[...]
    for t in range(CHUNK):
        y = yt_sc[:, SSM_GROUP * t:SSM_GROUP * (t + 1), :].reshape(GB_W, -1)
        v = y + d * ht_sc[t]
        z_ref[t] = jax.nn.gelu(v, approximate=True).T.astype(BF16)


def _s5_out_call(h_ph, xin_f, xin_b, mt, qt, d_col, nk):
    nc = h_ph.shape[0]
    xspec = pl.BlockSpec((nk, GROUP_BLOCK, SW), lambda gb, i: (i, gb, 0))
    wspec = pl.BlockSpec((GROUP_BLOCK, CW, CW), lambda gb, i: (gb, 0, 0))
    return pl.pallas_call(
        _s5_out_kernel,
        grid=(N_GB, nc // nk),
        in_specs=[*_h_specs(nk), xspec, xspec, wspec, wspec,
                  pl.BlockSpec((GB_W, nk), lambda gb, i: (gb, 0))],
        out_specs=pl.BlockSpec((CHUNK, nk, GB_W), lambda gb, i: (0, i, gb)),
        out_shape=jax.ShapeDtypeStruct((CHUNK, nc, D_MODEL), BF16),
        scratch_shapes=[
            pltpu.VMEM((GROUP_BLOCK, CW, nk), BF16),
            pltpu.VMEM((CHUNK, GB_W, nk), F32),
            pltpu.VMEM((GROUP_BLOCK, CW, nk), F32),
        ],
        compiler_params=_cparams(("parallel", "parallel")),
        name="s5_outputs",
    )(*([h_ph] * CHUNK), xin_f, xin_b, mt, qt, d_col)


def _s5_matrices(lam_re, lam_im, log_step, b_re, b_im, c_re, c_im):
    hp = lax.Precision.HIGHEST
    dt = jnp.exp(log_step.astype(F32))[..., None]
    lam_re = lam_re.astype(F32)
    lam_im = lam_im.astype(F32)
    mag = jnp.exp(lam_re * dt)
    ang = lam_im * dt
    lb_re = mag * jnp.cos(ang)
    lb_im = mag * jnp.sin(ang)
    den = lam_re * lam_re + lam_im * lam_im
    nr = lb_re - 1.0
    ni = lb_im
    coef_re = (nr * lam_re + ni * lam_im) / den
    coef_im = (ni * lam_re - nr * lam_im) / den
    b_re = b_re.astype(F32)
    b_im = b_im.astype(F32)
    bb_re = coef_re[..., None] * b_re - coef_im[..., None] * b_im
    bb_im = coef_re[..., None] * b_im + coef_im[..., None] * b_re
    c_re = c_re.astype(F32)
    c_im = c_im.astype(F32)

    pw_re = [jnp.ones_like(lb_re)]
    pw_im = [jnp.zeros_like(lb_im)]
    for _ in range(CHUNK):
        pr, pi = pw_re[-1], pw_im[-1]
        pw_re.append(pr * lb_re - pi * lb_im)
        pw_im.append(pr * lb_im + pi * lb_re)
    pw_re = jnp.stack(pw_re, axis=-1)
    pw_im = jnp.stack(pw_im, axis=-1)

    bbl_re = jnp.tile(bb_re, (1, 1, 1, CHUNK))
    bbl_im = jnp.tile(bb_im, (1, 1, 1, CHUNK))

    def response(pr, pi):
        prl = jnp.repeat(pr, SSM_GROUP, axis=-1)
        pil = jnp.repeat(pi, SSM_GROUP, axis=-1)
        return prl * bbl_re - pil * bbl_im, prl * bbl_im + pil * bbl_re

    w_re, w_im = response(pw_re[..., :CHUNK], pw_im[..., :CHUNK])
    wr_re, wr_im = response(pw_re[..., CHUNK - 1::-1], pw_im[..., CHUNK - 1::-1])

    kmat = (jnp.einsum('dgcn,dgnx->dgcx', c_re, w_re, precision=hp)
            - jnp.einsum('dgcn,dgnx->dgcx', c_im, w_im, precision=hp))
    lane = jnp.arange(CW)
    step, chan = lane // SSM_GROUP, lane % SSM_GROUP
    same_c = chan[:, None] == chan[None, :]
    t_i = jnp.arange(CHUNK)[:, None, None]
    sel_f = (same_c[None] & (step[None, :, None] == t_i - step[None, None, :])).astype(F32)
    sel_b = (same_c[None] & (step[None, :, None] == step[None, None, :] - t_i)).astype(F32)
    mt = (jnp.einsum('gcx,txy->gtcy', kmat[0], sel_f, precision=hp)
          + jnp.einsum('gcx,txy->gtcy', kmat[1], sel_b, precision=hp)).reshape(N_GROUPS, CW, CW)

    pt = jnp.stack([wr_re[0], wr_im[0], w_re[1], w_im[1]], axis=1).reshape(N_GROUPS, 2 * SW, CW)

    def carry(d, pr, pi):
        cr = c_re[d][:, None]
        ci = c_im[d][:, None]
        pr = pr.transpose(0, 2, 1)[:, :, None, :]
        pi = pi.transpose(0, 2, 1)[:, :, None, :]
        return [cr * pr - ci * pi, -(cr * pi + ci * pr)]

    qt = jnp.concatenate(carry(0, pw_re[0, ..., 1:], pw_im[0, ..., 1:])
                         + carry(1, pw_re[1, ..., :0:-1], pw_im[1, ..., :0:-1]), axis=-1)
    qt = qt.reshape(N_GROUPS, CW, 2 * SW)

    a_re = pw_re[..., CHUNK]
    a_im = pw_im[..., CHUNK]
    ar = jnp.concatenate([a_re, a_re], axis=-1)
    ai = jnp.concatenate([-a_im, a_im], axis=-1)
    return mt.astype(BF16), pt.astype(BF16), qt.astype(BF16), ar, ai


def _trunk(x, p, tm, nk):
    b, l, _ = x.shape
    nc = b * l // CHUNK
    kc = min(128, l // CHUNK)
    q_t, k, v_t = _qkv_call(x, p['g_mix0'], p['wqkv_t'], p['gq'], p['gk'], tm)
    o_t = _attn_call(q_t, k, v_t, p['hm'], p['bias'], p['rmask'])
    x_ph, h_ph, hb_ph = _attn_ffn_call(x, o_t, p['wo'], p['g_ffn0'], p['wgu0'], p['wd0'], p['g_mix1'], tm)

    sf, sb = _s5_state_call(hb_ph, p['pt'], nk)
    sshape = (b, l // CHUNK, N_GROUPS, SW)
    xin_f, xin_b = _scan_call(sf.reshape(sshape), sb.reshape(sshape), p['ar'], p['ai'], kc)
    z_pm = _s5_out_call(h_ph, xin_f.reshape(nc, N_GROUPS, SW), xin_b.reshape(nc, N_GROUPS, SW),
                        p['mt'], p['qt'], p['d_col'], nk)
    y = _glu_ffn_call(x_ph, z_pm, p['wglu'], p['g_ffn1'], p['wgu1'], p['wd1'], tm)
    return y.reshape(b, l, D_MODEL)


def kernel(x_prompt, x_sample, norm_mix, norm_ffn, w_qkv, w_o, q_gain, k_gain, rpb, lam_re, lam_im, log_step,
           b_re, b_im, c_re, c_im, d_skip, w_glu, w_gate_up, w_down):
    tm, nk = 512, 256
    scale = HEAD_DIM ** -0.5 * LOG2E
    mt, pt, qt, ar, ai = _s5_matrices(lam_re[0], lam_im[0], log_step[0], b_re[0], b_im[0], c_re[0], c_im[0])
    bias, rmask = _attn_tables(rpb[0])
    rows = np.arange(PAIR_W)[None, :, None]
    hm = np.broadcast_to((rows // HEAD_DIM) == np.arange(2)[:, None, None], (2, PAIR_W, QB))
    p = dict(
        g_mix0=norm_mix[0].astype(F32)[None], g_mix1=norm_mix[1].astype(F32)[None],
        g_ffn0=norm_ffn[0].astype(F32)[None], g_ffn1=norm_ffn[1].astype(F32)[None],
        wqkv_t=w_qkv[0].T.astype(BF16), wo=w_o[0].astype(BF16),
        gq=jnp.broadcast_to((q_gain[0].astype(F32) * scale)[:, None], (HEAD_DIM, tm // ROW_SPLIT)),
        gk=jnp.broadcast_to(k_gain[0].astype(F32)[:, None], (HEAD_DIM, tm // ROW_SPLIT)),
        hm=jnp.asarray(hm, BF16), bias=bias, rmask=rmask,
        wgu0=w_gate_up[0].astype(BF16), wd0=w_down[0].astype(BF16),
        wgu1=w_gate_up[1].astype(BF16), wd1=w_down[1].astype(BF16),
        wglu=w_glu[0].astype(BF16),
        mt=mt, pt=pt, qt=qt, ar=ar, ai=ai,
        d_col=jnp.broadcast_to(d_skip[0].astype(F32)[:, None], (D_MODEL, nk)),
    )
    return _trunk(x_prompt, p, tm, nk), _trunk(x_sample, p, tm, nk)
```

```python
import functools

import numpy as np
import jax
import jax.numpy as jnp
from jax import lax
from jax.experimental import pallas as pl
from jax.experimental.pallas import tpu as pltpu

D_MODEL = 1024
GRID_W = 64
N_HEADS = 16
HEAD_DIM = D_MODEL // N_HEADS
WIN_ROWS = 8
WIN_COLS = 16
SSM_GROUP = 16
N_GROUPS = D_MODEL // SSM_GROUP
STATE = 64
D_FF = 2816
EPS = 1e-6

F32 = jnp.float32
BF16 = jnp.bfloat16

V7X_LANES = 128
V7X_VMEM_BYTES = 64 * 1024 * 1024
VMEM_LIMIT = (V7X_VMEM_BYTES * 7) // 8

HEAD_PAIRS = N_HEADS // 2
PAIR_W = 2 * HEAD_DIM
Q_ROWS = 4
QB = Q_ROWS * GRID_W
KB = 3 * QB
NEG = -1e30
ONES_ROWS = 16
LOG2E = 1.4426950408889634
RM_ROWS = 16
TILE_SKIP, TILE_OPEN, TILE_MASKED = 0, 1, 2
HEADS_PER_TRIP = 8


def _tile_plans():
    assert 2 * Q_ROWS <= WIN_ROWS and V7X_LANES % GRID_W == 0
    per_tile = V7X_LANES // GRID_W
    interior, edge = [], []
    for kr in range(3 * Q_ROWS):
        irow, erow = [], []
        for c in range(QB // V7X_LANES):
            ok = [r <= kr < r + WIN_ROWS for r in range(per_tile * c, per_tile * (c + 1))]
            irow.append(TILE_OPEN if all(ok) else TILE_MASKED if any(ok) else TILE_SKIP)
            erow.append(TILE_OPEN if Q_ROWS <= kr < 2 * Q_ROWS else TILE_MASKED)
        interior.append(irow)
        edge.append(erow)
    return interior, edge


_INTERIOR_PLAN, _EDGE_PLAN = _tile_plans()

CHUNK = 16
CW = CHUNK * SSM_GROUP
SW = 2 * STATE
GROUP_BLOCK = 16
GB_W = GROUP_BLOCK * SSM_GROUP
N_GB = N_GROUPS // GROUP_BLOCK
ROW_SPLIT = 2


def _cparams(sem):
    return pltpu.CompilerParams(dimension_semantics=sem, vmem_limit_bytes=VMEM_LIMIT)


def _rms(x, g):
    return x * lax.rsqrt(jnp.mean(x * x, axis=-1, keepdims=True) + EPS) * g


def _dot(a, b):
    return jnp.dot(a, b, preferred_element_type=F32)


def _dot_nt(a, b):
    return lax.dot_general(a, b, (((1,), (1,)), ((), ())), preferred_element_type=F32)


def _dot_tn(a, b):
    return lax.dot_general(a, b, (((0,), (0,)), ((), ())), preferred_element_type=F32)


def _qkv_kernel(x_ref, g_ref, w_ref, gq_ref, gk_ref, q_ref, k_ref, v_ref):
    tm = x_ref.shape[1] // ROW_SPLIT

    def head_norm(t, gain):
        t3 = t.reshape(N_HEADS, HEAD_DIM, tm)
        y = t3 * lax.rsqrt(jnp.mean(t3 * t3, axis=1, keepdims=True) + EPS)
        return y * gain[None]

    for r in range(ROW_SPLIT):
        tok = slice(r * tm, (r + 1) * tm)
        h = _rms(x_ref[0, tok, :], g_ref[...]).astype(BF16)
        q = head_norm(_dot_nt(w_ref[0:D_MODEL, :], h), gq_ref[...])
        k = head_norm(_dot_nt(w_ref[D_MODEL:2 * D_MODEL, :], h), gk_ref[...])
        v = _dot_nt(w_ref[2 * D_MODEL:3 * D_MODEL, :], h)
        q_ref[0, r] = q.reshape(HEAD_PAIRS, PAIR_W, tm).astype(BF16)
        v_ref[0, r] = v.reshape(HEAD_PAIRS, PAIR_W, tm).astype(BF16)
        kt = k.reshape(D_MODEL, tm).T
        for j in range(HEAD_PAIRS):
            k_ref[0, j, tok, :] = kt[:, PAIR_W * j:PAIR_W * (j + 1)].astype(BF16)


def _qkv_call(x, g, w_t, gq, gk, tm):
    b, l, _ = x.shape
    assert tm // ROW_SPLIT == QB
    blocked = pl.BlockSpec((1, ROW_SPLIT, HEAD_PAIRS, PAIR_W, QB), lambda i, j: (i, j, 0, 0, 0))
    return pl.pallas_call(
        _qkv_kernel,
        grid=(b, l // tm),
        in_specs=[
            pl.BlockSpec((1, tm, D_MODEL), lambda i, j: (i, j, 0)),
            pl.BlockSpec((1, D_MODEL), lambda i, j: (0, 0)),
            pl.BlockSpec((3 * D_MODEL, D_MODEL), lambda i, j: (0, 0)),
            pl.BlockSpec((HEAD_DIM, tm // ROW_SPLIT), lambda i, j: (0, 0)),
            pl.BlockSpec((HEAD_DIM, tm // ROW_SPLIT), lambda i, j: (0, 0)),
        ],
        out_specs=[
            blocked,
            pl.BlockSpec((1, HEAD_PAIRS, tm, PAIR_W), lambda i, j: (i, 0, j, 0)),
            blocked,
        ],
        out_shape=[
            jax.ShapeDtypeStruct((b, l // QB, HEAD_PAIRS, PAIR_W, QB), BF16),
            jax.ShapeDtypeStruct((b, HEAD_PAIRS, l, PAIR_W), BF16),
            jax.ShapeDtypeStruct((b, l // QB, HEAD_PAIRS, PAIR_W, QB), BF16),
        ],
        compiler_params=_cparams(("parallel", "parallel")),
        name="qkv_proj",
    )(x, g, w_t, gq, gk)


def _attn_kernel(q_ref, kp_ref, kc_ref, kn_ref, vp_ref, vc_ref, vn_ref, hm_ref, bias_ref, rm_ref, o_ref,
                 sa_sc, sb_sc, ma_sc, mb_sc):
    ones = jnp.ones((ONES_ROWS, QB), BF16)
    zero_tile = jnp.zeros((GRID_W, V7X_LANES), BF16)
    lane_tiles = QB // V7X_LANES

    def tile(kr, c):
        return slice(GRID_W * kr, GRID_W * (kr + 1)), slice(V7X_LANES * c, V7X_LANES * (c + 1))

    def scores(j, e, s_sc, m_sc, plan):
        qm = q_ref[0, 0, j] * hm_ref[e]
        mx = [None] * lane_tiles
        for blk, k_ref in enumerate((kp_ref, kc_ref, kn_ref)):
            used = [kl for kl in range(Q_ROWS) if any(t != TILE_SKIP for t in plan[Q_ROWS * blk + kl])]
            k0 = used[0]
            d = _dot(k_ref[0, j, GRID_W * k0:GRID_W * (used[-1] + 1), :], qm)
            for kl in used:
                kr = Q_ROWS * blk + kl
                for c in range(lane_tiles):
                    if plan[kr][c] == TILE_SKIP:
                        continue
                    rows, lanes = tile(kr, c)
                    s = d[GRID_W * (kl - k0):GRID_W * (kl - k0 + 1), lanes] + bias_ref[j, e, rows, lanes]
                    if plan[kr][c] == TILE_MASKED:
                        s = s + rm_ref[0, kr:kr + 1, lanes]
                    s_sc[rows, lanes] = s
                    mx[c] = s if mx[c] is None else jnp.maximum(mx[c], s)
        m_sc[...] = jnp.concatenate(mx, axis=1)

    def finish(j, e, s_sc, m_sc, plan):
        lo, hi = HEAD_DIM * e, HEAD_DIM * (e + 1)
        m = jnp.max(m_sc[...], axis=0, keepdims=True)
        o = None
        for blk, v_ref in enumerate((vp_ref, vc_ref, vn_ref)):
            p_rows = []
            for kl in range(Q_ROWS):
                kr = Q_ROWS * blk + kl
                p_tiles = []
                for c in range(lane_tiles):
                    rows, lanes = tile(kr, c)
                    p_tiles.append(zero_tile if plan[kr][c] == TILE_SKIP else
                                   jnp.exp2(s_sc[rows, lanes] - m[:, lanes]).astype(BF16))
                p_rows.append(jnp.concatenate(p_tiles, axis=1))
            pb = jnp.concatenate(p_rows, axis=0)
            v = jnp.concatenate([v_ref[0, 0, j, lo:hi, :], ones], axis=0)
            ob = _dot(v, pb)
            o = ob if o is None else o + ob
        o_ref[0, 0, j, lo:hi, :] = (o[:HEAD_DIM] / o[HEAD_DIM:HEAD_DIM + 1]).astype(BF16)

    def all_heads(plan):
        bufs = ((sa_sc, ma_sc), (sb_sc, mb_sc))

        def stage(pair0, t):
            scores(pair0 + (t + 1) // 2, (t + 1) % 2, *bufs[(t + 1) % 2], plan)
            finish(pair0 + t // 2, t % 2, *bufs[t % 2], plan)

        scores(0, 0, *bufs[0], plan)

        def body(i, carry):
            for t in range(HEADS_PER_TRIP):
                stage((HEADS_PER_TRIP // 2) * i, t)
            return carry

        trips = (N_HEADS - 1) // HEADS_PER_TRIP
        lax.fori_loop(0, trips, body, 0)
        for t in range(trips * HEADS_PER_TRIP, N_HEADS - 1):
            stage(0, t)
        finish(HEAD_PAIRS - 1, 1, *bufs[1], plan)

    interior = jnp.logical_and(pl.program_id(1) > 0, pl.program_id(1) < pl.num_programs(1) - 1)
    pl.when(interior)(lambda: all_heads(_INTERIOR_PLAN))
    pl.when(jnp.logical_not(interior))(lambda: all_heads(_EDGE_PLAN))


def _attn_call(q_t, k, v_t, hm, bias, rmask):
    b, nblk = q_t.shape[:2]
    prev = lambda j: jnp.maximum(j - 1, 0)
    nxt = lambda j: jnp.minimum(j + 1, nblk - 1)
    variant = lambda j: (j > 0).astype(jnp.int32) + (j == nblk - 1).astype(jnp.int32)
    kspec = lambda f: pl.BlockSpec((1, HEAD_PAIRS, QB, PAIR_W), lambda i, j: (i, 0, f(j), 0))
    vspec = lambda f: pl.BlockSpec((1, 1, HEAD_PAIRS, PAIR_W, QB), lambda i, j: (i, f(j), 0, 0, 0))
    same = lambda j: j
    return pl.pallas_call(
        _attn_kernel,
        grid=(b, nblk),
        in_specs=[
            vspec(same),
            kspec(prev), kspec(same), kspec(nxt),
            vspec(prev), vspec(same), vspec(nxt),
            pl.BlockSpec((2, PAIR_W, QB), lambda i, j: (0, 0, 0)),
            pl.BlockSpec((HEAD_PAIRS, 2, KB, QB), lambda i, j: (0, 0, 0, 0), pipeline_mode=pl.Buffered(1)),
            pl.BlockSpec((1, RM_ROWS, QB), lambda i, j: (variant(j), 0, 0)),
        ],
        out_specs=vspec(same),
        out_shape=jax.ShapeDtypeStruct(q_t.shape, BF16),
        scratch_shapes=[pltpu.VMEM((KB, QB), F32)] * 2 + [pltpu.VMEM((GRID_W, QB), F32)] * 2,
        compiler_params=_cparams(("parallel", "parallel")),
        name="nbr_attention",
    )(q_t, k, k, k, v_t, v_t, v_t, hm, bias, rmask)


def _attn_tables(rpb):
    kr = np.arange(3 * Q_ROWS)[:, None, None, None]
    kc = np.arange(GRID_W)[None, :, None, None]
    r = np.arange(Q_ROWS)[None, None, :, None]
    c = np.arange(GRID_W)[None, None, None, :]
    shape = (3 * Q_ROWS, GRID_W, Q_ROWS, GRID_W)
    cs = np.clip(c - WIN_COLS // 2, 0, GRID_W - WIN_COLS)
    col_ok = ((kc >= cs) & (kc < cs + WIN_COLS))[0, :, 0, :]
    dc = (kc - c + WIN_COLS - 1)[0, :, 0, :]
    onehot = ((dc[None] == np.arange(2 * WIN_COLS - 1)[:, None, None]) & col_ok[None]).astype(np.float32)
    toep = jnp.einsum('hrd,dkc->hrkc', rpb.astype(F32), onehot, precision=lax.Precision.HIGHEST)
    toep = jnp.where(col_ok[None, None], toep * LOG2E, NEG)
    off = WIN_ROWS - 1 - Q_ROWS
    bias = jnp.concatenate([toep[:, off - rr:off - rr + 3 * Q_ROWS] for rr in range(Q_ROWS)], axis=-1)
    bias = bias.reshape(HEAD_PAIRS, 2, KB, QB)
    rshape = (3 * Q_ROWS, 1, Q_ROWS, GRID_W)
    first = np.broadcast_to(kr >= Q_ROWS, rshape)
    inner = np.broadcast_to((kr >= r) & (kr < r + WIN_ROWS), rshape)
    last = np.broadcast_to(kr < 2 * Q_ROWS, rshape)
    rmask = np.where(np.stack([first, inner, last]), 0.0, NEG).astype(np.float32).reshape(3, 3 * Q_ROWS, QB)
    rmask = np.pad(rmask, ((0, 0), (0, RM_ROWS - 3 * Q_ROWS), (0, 0)))
    return bias, jnp.asarray(rmask)


def _ffn(x0, gf_ref, wgu_ref, wd_ref):
    hb = _rms(x0, gf_ref[...]).astype(BF16)
    g = _dot(hb, wgu_ref[:, :D_FF])
    u = _dot(hb, wgu_ref[:, D_FF:])
    act = (g * (1.0 / (1.0 + jnp.exp(-g))) * u).astype(BF16)
    return x0 + _dot(act, wd_ref[...])


def _resident(shape):
    return pl.BlockSpec(shape, lambda *idx: (0,) * len(shape), pipeline_mode=pl.Buffered(1))


def _phase_major(x):
    return jnp.swapaxes(x.reshape(-1, CHUNK, D_MODEL), 0, 1)


def _store_chunk_major(refs, x):
    xs = _phase_major(x)
    for s in range(CHUNK):
        for ref in refs:
            ref[:, D_MODEL * s:D_MODEL * (s + 1)] = xs[s].astype(ref.dtype)


def _attn_ffn_kernel(x_ref, o_ref, wo_ref, gf_ref, wgu_ref, wd_ref, gn_ref, x1_ref, hb_ref):
    parts = []
    for r in range(ROW_SPLIT):
        o = o_ref[0, r].reshape(D_MODEL, QB)
        x0 = x_ref[0, QB * r:QB * (r + 1), :] + _dot_tn(o, wo_ref[...])
        parts.append(_ffn(x0, gf_ref, wgu_ref, wd_ref))
    x1 = jnp.concatenate(parts, axis=0)
    _store_chunk_major([x1_ref], x1)
    _store_chunk_major([hb_ref], _rms(x1, gn_ref[...]))


def _attn_ffn_call(x, o_t, wo, gf, wgu, wd, gn, tm):
    b, l, _ = x.shape
    nt = l // tm
    out = pl.BlockSpec((tm // CHUNK, CHUNK * D_MODEL), lambda i, j: (i * nt + j, 0))
    return pl.pallas_call(
        _attn_ffn_kernel,
        grid=(b, nt),
        in_specs=[
            pl.BlockSpec((1, tm, D_MODEL), lambda i, j: (i, j, 0)),
            pl.BlockSpec((1, ROW_SPLIT, HEAD_PAIRS, PAIR_W, QB), lambda i, j: (i, j, 0, 0, 0)),
            _resident((D_MODEL, D_MODEL)),
            _resident((1, D_MODEL)),
            _resident((D_MODEL, 2 * D_FF)),
            _resident((D_FF, D_MODEL)),
            _resident((1, D_MODEL)),
        ],
        out_specs=[out, out],
        out_shape=[jax.ShapeDtypeStruct((b * l // CHUNK, CHUNK * D_MODEL), dt) for dt in (F32, BF16)],
        compiler_params=_cparams(("parallel", "parallel")),
        name="attn_out_ffn",
    )(x, o_t, wo, gf, wgu, wd, gn)


def _glu_ffn_kernel(x_ref, z_ref, wglu_ref, gf_ref, wgu_ref, wd_ref, y_ref):
    parts = []
    for r in range(ROW_SPLIT):
        ph = range(r * CHUNK // ROW_SPLIT, (r + 1) * CHUNK // ROW_SPLIT)
        z = z_ref[ph.start:ph.stop].reshape(-1, D_MODEL)
        x = jnp.concatenate([x_ref[:, D_MODEL * s:D_MODEL * (s + 1)] for s in ph], axis=0)
        zz = _dot(z, wglu_ref[...])
        za, zg = zz[:, :D_MODEL], zz[:, D_MODEL:]
        x0 = x + za * (1.0 / (1.0 + jnp.exp(-zg)))
        parts.append(_ffn(x0, gf_ref, wgu_ref, wd_ref))
    y = jnp.concatenate(parts, axis=0).reshape(CHUNK, -1, D_MODEL)
    y_ref[...] = jnp.swapaxes(y, 0, 1).reshape(-1, D_MODEL)


def _glu_ffn_call(x_ph, z_pm, wglu, gf, wgu, wd, tm):
    nc = x_ph.shape[0]
    nk = tm // CHUNK
    return pl.pallas_call(
        _glu_ffn_kernel,
        grid=(nc // nk,),
        in_specs=[
            pl.BlockSpec((nk, CHUNK * D_MODEL), lambda i: (i, 0)),
            pl.BlockSpec((CHUNK, nk, D_MODEL), lambda i: (0, i, 0)),
            _resident((D_MODEL, 2 * D_MODEL)),
            _resident((1, D_MODEL)),
            _resident((D_MODEL, 2 * D_FF)),
            _resident((D_FF, D_MODEL)),
        ],
        out_specs=pl.BlockSpec((tm, D_MODEL), lambda i: (i, 0)),
        out_shape=jax.ShapeDtypeStruct((nc * CHUNK, D_MODEL), F32),
        compiler_params=_cparams(("parallel",)),
        name="glu_out_ffn",
    )(x_ph, z_pm, wglu, gf, wgu, wd)


def _chunk_operands(h_refs, ut_sc, ht_sc):
    for s, h_ref in enumerate(h_refs):
        ht = h_ref[...].T
        if ht_sc is not None:
            ht_sc[s] = ht
        ut_sc[:, SSM_GROUP * s:SSM_GROUP * (s + 1), :] = (
            ht.reshape(GROUP_BLOCK, SSM_GROUP, -1).astype(BF16))


def _s5_state_kernel(*refs):
    h_refs = refs[:CHUNK]
    pt_ref, sf_ref, sb_ref, ut_sc, st_sc = refs[CHUNK:]
    _chunk_operands(h_refs, ut_sc, None)
    for g in range(GROUP_BLOCK):
        st_sc[g] = _dot(pt_ref[g], ut_sc[g]).T
    sf_ref[...] = jnp.swapaxes(st_sc[:, :, :SW], 0, 1)
    sb_ref[...] = jnp.swapaxes(st_sc[:, :, SW:], 0, 1)


def _h_specs(nk):
    return [pl.BlockSpec((nk, GB_W), lambda gb, i, s=s: (i, s * N_GB + gb)) for s in range(CHUNK)]


def _s5_state_call(h_ph, pt, nk):
    nc = h_ph.shape[0]
    sspec = pl.BlockSpec((nk, GROUP_BLOCK, SW), lambda gb, i: (i, gb, 0))
    return pl.pallas_call(
        _s5_state_kernel,
        grid=(N_GB, nc // nk),
        in_specs=[*_h_specs(nk),
                  pl.BlockSpec((GROUP_BLOCK, 2 * SW, CW), lambda gb, i: (gb, 0, 0))],
        out_specs=[sspec, sspec],
        out_shape=[jax.ShapeDtypeStruct((nc, N_GROUPS, SW), F32)] * 2,
        scratch_shapes=[pltpu.VMEM((GROUP_BLOCK, CW, nk), BF16),
                        pltpu.VMEM((GROUP_BLOCK, nk, 2 * SW), F32)],
        compiler_params=_cparams(("parallel", "parallel")),
        name="s5_chunk_states",
    )(*([h_ph] * CHUNK), pt)


def _scan_kernel(sf_ref, sb_ref, ar_ref, ai_ref, xf_ref, xb_ref, st_sc):
    kc = sf_ref.shape[1]

    @pl.when(pl.program_id(1) == 0)
    def _():
        st_sc[...] = jnp.zeros_like(st_sc)

    a = [(ar_ref[d], ai_ref[d]) for d in range(2)]

    def step(d, s_ref, x_ref, k, x, xs):
        ar, ai = a[d]
        s = s_ref[0, k]
        x_ref[0, k] = x.astype(BF16)
        return ar * x + ai * xs + s, ar * xs - ai * x + pltpu.roll(s, STATE, 1)

    def body(i, carry):
        xf, xfs, xb, xbs = carry
        return (*step(0, sf_ref, xf_ref, i, xf, xfs), *step(1, sb_ref, xb_ref, kc - 1 - i, xb, xbs))

    init = (st_sc[0, 0], st_sc[0, 1], st_sc[1, 0], st_sc[1, 1])
    xf, xfs, xb, xbs = lax.fori_loop(0, kc, body, init, unroll=8)
    st_sc[0, 0], st_sc[0, 1], st_sc[1, 0], st_sc[1, 1] = xf, xfs, xb, xbs


def _scan_call(sf, sb, ar, ai, kc):
    b, ncs = sf.shape[:2]
    nkb = ncs // kc
    fwd = pl.BlockSpec((1, kc, N_GROUPS, SW), lambda i, j: (i, j, 0, 0))
    bwd = pl.BlockSpec((1, kc, N_GROUPS, SW), lambda i, j: (i, nkb - 1 - j, 0, 0))
    tab = pl.BlockSpec((2, N_GROUPS, SW), lambda i, j: (0, 0, 0))
    return pl.pallas_call(
        _scan_kernel,
        grid=(b, nkb),
        in_specs=[fwd, bwd, tab, tab],
        out_specs=[fwd, bwd],
        out_shape=[jax.ShapeDtypeStruct(sf.shape, BF16)] * 2,
        scratch_shapes=[pltpu.VMEM((2, 2, N_GROUPS, SW), F32)],
        compiler_params=_cparams(("parallel", "arbitrary")),
        name="s5_chunk_scan",
    )(sf, sb, ar, ai)


def _s5_out_kernel(*refs):
    h_refs = refs[:CHUNK]
    xf_ref, xb_ref, mt_ref, qt_ref, z_ref, ut_sc, yt_sc = refs[CHUNK:]
    _chunk_operands(h_refs, ut_sc, None)
    xf = jnp.swapaxes(xf_ref[...].astype(F32), 0, 1)
    xb = jnp.swapaxes(xb_ref[...].astype(F32), 0, 1)
    for g in range(GROUP_BLOCK):
        xin = jnp.concatenate([xf[g], xb[g]], axis=1).astype(BF16)
        yt_sc[g] = _dot(mt_ref[g], ut_sc[g]) + _dot_nt(qt_ref[g], xin)
    for t in range(CHUNK):
        y = yt_sc[:, SSM_GROUP * t:SSM_GROUP * (t + 1), :].reshape(GB_W, -1)
        z_ref[t] = jax.nn.gelu(y, approximate=True).T.astype(BF16)


def _s5_out_call(h_ph, xin_f, xin_b, mt, qt, nk):
    nc = h_ph.shape[0]
    xspec = pl.BlockSpec((nk, GROUP_BLOCK, SW), lambda gb, i: (i, gb, 0))
    wspec = pl.BlockSpec((GROUP_BLOCK, CW, CW), lambda gb, i: (gb, 0, 0))
    return pl.pallas_call(
        _s5_out_kernel,
        grid=(N_GB, nc // nk),
        in_specs=[*_h_specs(nk), xspec, xspec, wspec, wspec],
        out_specs=pl.BlockSpec((CHUNK, nk, GB_W), lambda gb, i: (0, i, gb)),
        out_shape=jax.ShapeDtypeStruct((CHUNK, nc, D_MODEL), BF16),
        scratch_shapes=[
            pltpu.VMEM((GROUP_BLOCK, CW, nk), BF16),
            pltpu.VMEM((GROUP_BLOCK, CW, nk), F32),
        ],
        compiler_params=_cparams(("parallel", "parallel")),
        name="s5_outputs",
    )(*([h_ph] * CHUNK), xin_f, xin_b, mt, qt)


def _s5_matrices(lam_re, lam_im, log_step, b_re, b_im, c_re, c_im, d_skip):
    hp = lax.Precision.HIGHEST
    dt = jnp.exp(log_step.astype(F32))[..., None]
    lam_re = lam_re.astype(F32)
    lam_im = lam_im.astype(F32)
    mag = jnp.exp(lam_re * dt)
    ang = lam_im * dt
    lb_re = mag * jnp.cos(ang)
    lb_im = mag * jnp.sin(ang)
    den = lam_re * lam_re + lam_im * lam_im
    nr = lb_re - 1.0
    ni = lb_im
    coef_re = (nr * lam_re + ni * lam_im) / den
    coef_im = (ni * lam_re - nr * lam_im) / den
    b_re = b_re.astype(F32)
    b_im = b_im.astype(F32)
    bb_re = coef_re[..., None] * b_re - coef_im[..., None] * b_im
    bb_im = coef_re[..., None] * b_im + coef_im[..., None] * b_re
    c_re = c_re.astype(F32)
    c_im = c_im.astype(F32)

    pw_re = [jnp.ones_like(lb_re)]
    pw_im = [jnp.zeros_like(lb_im)]
    for _ in range(CHUNK):
        pr, pi = pw_re[-1], pw_im[-1]
        pw_re.append(pr * lb_re - pi * lb_im)
        pw_im.append(pr * lb_im + pi * lb_re)
    pw_re = jnp.stack(pw_re, axis=-1)
    pw_im = jnp.stack(pw_im, axis=-1)

    bbl_re = jnp.tile(bb_re, (1, 1, 1, CHUNK))
    bbl_im = jnp.tile(bb_im, (1, 1, 1, CHUNK))

    def response(pr, pi):
        prl = jnp.repeat(pr, SSM_GROUP, axis=-1)
        pil = jnp.repeat(pi, SSM_GROUP, axis=-1)
        return prl * bbl_re - pil * bbl_im, prl * bbl_im + pil * bbl_re

    w_re, w_im = response(pw_re[..., :CHUNK], pw_im[..., :CHUNK])
    wr_re, wr_im = response(pw_re[..., CHUNK - 1::-1], pw_im[..., CHUNK - 1::-1])

    kmat = (jnp.einsum('dgcn,dgnx->dgcx', c_re, w_re, precision=hp)
            - jnp.einsum('dgcn,dgnx->dgcx', c_im, w_im, precision=hp))
    lane = jnp.arange(CW)
    step, chan = lane // SSM_GROUP, lane % SSM_GROUP
    same_c = chan[:, None] == chan[None, :]
    t_i = jnp.arange(CHUNK)[:, None, None]
    sel_f = (same_c[None] & (step[None, :, None] == t_i - step[None, None, :])).astype(F32)
    sel_b = (same_c[None] & (step[None, :, None] == step[None, None, :] - t_i)).astype(F32)
    mt = (jnp.einsum('gcx,txy->gtcy', kmat[0], sel_f, precision=hp)
          + jnp.einsum('gcx,txy->gtcy', kmat[1], sel_b, precision=hp)).reshape(N_GROUPS, CW, CW)
    d_diag = jnp.tile(d_skip.astype(F32).reshape(N_GROUPS, SSM_GROUP), (1, CHUNK))
    mt = mt + jnp.eye(CW, dtype=F32)[None] * d_diag[:, :, None]

    pt = jnp.stack([wr_re[0], wr_im[0], w_re[1], w_im[1]], axis=1).reshape(N_GROUPS, 2 * SW, CW)

    def carry(d, pr, pi):
        cr = c_re[d][:, None]
        ci = c_im[d][:, None]
        pr = pr.transpose(0, 2, 1)[:, :, None, :]
        pi = pi.transpose(0, 2, 1)[:, :, None, :]
        return [cr * pr - ci * pi, -(cr * pi + ci * pr)]

    qt = jnp.concatenate(carry(0, pw_re[0, ..., 1:], pw_im[0, ..., 1:])
                         + carry(1, pw_re[1, ..., :0:-1], pw_im[1, ..., :0:-1]), axis=-1)
    qt = qt.reshape(N_GROUPS, CW, 2 * SW)

    a_re = pw_re[..., CHUNK]
    a_im = pw_im[..., CHUNK]
    ar = jnp.concatenate([a_re, a_re], axis=-1)
    ai = jnp.concatenate([-a_im, a_im], axis=-1)
    return mt.astype(BF16), pt.astype(BF16), qt.astype(BF16), ar, ai


def _trunk(x, p, tm, nk):
    b, l, _ = x.shape
    nc = b * l // CHUNK
    kc = min(128, l // CHUNK)
    q_t, k, v_t = _qkv_call(x, p['g_mix0'], p['wqkv_t'], p['gq'], p['gk'], tm)
    o_t = _attn_call(q_t, k, v_t, p['hm'], p['bias'], p['rmask'])
    x_ph, hb_ph = _attn_ffn_call(x, o_t, p['wo'], p['g_ffn0'], p['wgu0'], p['wd0'], p['g_mix1'], tm)

    sf, sb = _s5_state_call(hb_ph, p['pt'], nk)
    sshape = (b, l // CHUNK, N_GROUPS, SW)
    xin_f, xin_b = _scan_call(sf.reshape(sshape), sb.reshape(sshape), p['ar'], p['ai'], kc)
    z_pm = _s5_out_call(hb_ph, xin_f.reshape(nc, N_GROUPS, SW), xin_b.reshape(nc, N_GROUPS, SW),
                        p['mt'], p['qt'], nk)
    y = _glu_ffn_call(x_ph, z_pm, p['wglu'], p['g_ffn1'], p['wgu1'], p['wd1'], tm)
    return y.reshape(b, l, D_MODEL)


def kernel(x_prompt, x_sample, norm_mix, norm_ffn, w_qkv, w_o, q_gain, k_gain, rpb, lam_re, lam_im, log_step,
           b_re, b_im, c_re, c_im, d_skip, w_glu, w_gate_up, w_down):
    tm, nk = 512, 256
    scale = HEAD_DIM ** -0.5 * LOG2E
    mt, pt, qt, ar, ai = _s5_matrices(lam_re[0], lam_im[0], log_step[0], b_re[0], b_im[0], c_re[0], c_im[0],
                                      d_skip[0])
    bias, rmask = _attn_tables(rpb[0])
    rows = np.arange(PAIR_W)[None, :, None]
    hm = np.broadcast_to((rows // HEAD_DIM) == np.arange(2)[:, None, None], (2, PAIR_W, QB))
    p = dict(
        g_mix0=norm_mix[0].astype(F32)[None], g_mix1=norm_mix[1].astype(F32)[None],
        g_ffn0=norm_ffn[0].astype(F32)[None], g_ffn1=norm_ffn[1].astype(F32)[None],
        wqkv_t=w_qkv[0].T.astype(BF16), wo=w_o[0].astype(BF16),
        gq=jnp.broadcast_to((q_gain[0].astype(F32) * scale)[:, None], (HEAD_DIM, tm // ROW_SPLIT)),
        gk=jnp.broadcast_to(k_gain[0].astype(F32)[:, None], (HEAD_DIM, tm // ROW_SPLIT)),
        hm=jnp.asarray(hm, BF16), bias=bias, rmask=rmask,
        wgu0=w_gate_up[0].astype(BF16), wd0=w_down[0].astype(BF16),
        wgu1=w_gate_up[1].astype(BF16), wd1=w_down[1].astype(BF16),
        wglu=w_glu[0].astype(BF16),
        mt=mt, pt=pt, qt=qt, ar=ar, ai=ai,
    )
    return _trunk(x_prompt, p, tm, nk), _trunk(x_sample, p, tm, nk)
```
